```python
import jax, jax.numpy as jnp
from jax import lax
import numpy as np

D_MODEL = 2048
BATCH = 8
SEQ = 2048
DEPTH = 1
DEC_BATCH = 128
DEC_SEQ = 4
PAST_LEN = 2048
PAGE_SIZE = 128

N_META = 16
D_ATTN = D_MODEL // 2
HEAD_DIM = 128
N_HEADS = D_ATTN // HEAD_DIM
D_CONV = D_MODEL // 2
CONV_WIDTH = 3
Q_BLOCK = 128
RMS_EPS = 1e-6
SB_BIAS_INIT = -7.0
SPLIT_SIZES = (D_ATTN, D_ATTN, D_ATTN, D_ATTN, D_CONV, D_CONV, D_CONV, D_CONV, D_MODEL, D_MODEL)
IN_COLS = sum(SPLIT_SIZES)

kernel_name = "stickbreak_shortconv_parallel_hybrid_step"


def rmsnorm(x, g):
    xf = x.astype(jnp.float32)
    y = xf * lax.rsqrt(jnp.mean(xf * xf, axis=-1, keepdims=True) + RMS_EPS)
    return (y * g.astype(jnp.float32)).astype(x.dtype)


def split_offsets():
    return [int(v) for v in np.cumsum(SPLIT_SIZES)[:-1]]


def stick_breaking(q, k, v, q_pos, k_pos, b_sb):
    z = (jnp.einsum('bqhd,bkhd->bhqk', q, k).astype(jnp.float32) * (HEAD_DIM ** -0.5)
         + b_sb.astype(jnp.float32)[None, :, None, None])
    mask = k_pos[None, :] < q_pos[:, None]
    log_keep = jnp.where(mask, jax.nn.log_sigmoid(-z), 0.0)
    after = lax.cumsum(log_keep, axis=3, reverse=True) - log_keep
    attn = jnp.where(mask, jnp.exp(jax.nn.log_sigmoid(z) + after), 0.0)
    return jnp.einsum('bhqk,bkhd->bqhd', attn.astype(v.dtype), v)


def attend_prompt(q, k, v, b_sb):
    b, t = q.shape[0], q.shape[1]
    nqb = -(-t // Q_BLOCK)
    pad = nqb * Q_BLOCK - t
    qp = jnp.pad(q, ((0, 0), (0, pad), (0, 0), (0, 0)))
    qb = qp.reshape(b, nqb, Q_BLOCK, N_HEADS, HEAD_DIM).transpose(1, 0, 2, 3, 4)
    q_pos = jnp.arange(nqb * Q_BLOCK).reshape(nqb, Q_BLOCK)
    k_pos = jnp.arange(t)
    out = lax.map(lambda a: stick_breaking(a[0], k, v, a[1], k_pos, b_sb), (qb, q_pos))
    return out.transpose(1, 0, 2, 3, 4).reshape(b, nqb * Q_BLOCK, N_HEADS, HEAD_DIM)[:, :t]


def causal_conv(u_ext, w, b):
    length = u_ext.shape[1] - (CONV_WIDTH - 1)
    y = b
    for i in range(CONV_WIDTH):
        y = y + w[i] * u_ext[:, i:i + length]
    return y


def mixer_layer(x, conv_hist, attend, g_pre, w_in, b_sb, b_gate, w_conv, b_conv, w_pa, w_pc, w_out, g_post):
    bsz, t = x.shape[0], x.shape[1]
    h = rmsnorm(x, g_pre)
    q, k, v, ga, cb, cc, ch, gc, ma, mc = jnp.split(h @ w_in, split_offsets(), axis=-1)
    q = q.reshape(bsz, t, N_HEADS, HEAD_DIM)
    k = k.reshape(bsz, t, N_HEADS, HEAD_DIM)
    v = v.reshape(bsz, t, N_HEADS, HEAD_DIM)
    a = attend(q, k, v, b_sb).reshape(bsz, t, D_ATTN) * jax.nn.silu(ga)
    u = cc * ch
    u_ext = jnp.concatenate([conv_hist.astype(u.dtype), u], axis=1)
    c = cb * causal_conv(u_ext, w_conv, b_conv) * jax.nn.silu(gc)
    merged = (jax.nn.sigmoid(ma + b_gate[:D_MODEL]) * (a @ w_pa)
              + jax.nn.sigmoid(mc + b_gate[D_MODEL:]) * (c @ w_pc))
    y = x + rmsnorm(merged @ w_out, g_post)
    return y, k, v, u_ext[:, -(CONV_WIDTH - 1):]


def setup_inputs(seed: int = 0) -> dict:
    key = jax.random.key(seed)
    ks = jax.random.split(key, 20)
    n_pages = PAST_LEN // PAGE_SIZE
    n_used = DEC_BATCH * n_pages
    n_pool = n_used + (n_used + 3) // 4
    perm = jax.random.permutation(ks[0], n_pool)
    page_table = perm[:n_used].reshape(DEC_BATCH, n_pages).astype(jnp.int32)
    nrm = lambda k, s, sc: jax.random.normal(k, s, jnp.float32) * sc
    return {
        "x_prompt": nrm(ks[1], (BATCH, SEQ, D_MODEL), 1.0),
        "x_sample": nrm(ks[2], (DEC_BATCH, DEC_SEQ, D_MODEL), 1.0),
        "cache_k": nrm(ks[3], (DEPTH, n_pool, PAGE_SIZE, N_HEADS, HEAD_DIM), 1.0),
        "cache_v": nrm(ks[4], (DEPTH, n_pool, PAGE_SIZE, N_HEADS, HEAD_DIM), 1.0),
        "state_conv": nrm(ks[5], (DEPTH, DEC_BATCH, CONV_WIDTH - 1, D_CONV), 1.0),
        "page_table": page_table,
        "meta_tokens": nrm(ks[6], (N_META, D_MODEL), 1.0),
        "g_pre": 1.0 + nrm(ks[7], (DEPTH, D_MODEL), 0.02),
        "w_in": nrm(ks[8], (DEPTH, D_MODEL, IN_COLS), D_MODEL ** -0.5),
        "b_sb": SB_BIAS_INIT + nrm(ks[16], (DEPTH, N_HEADS), 0.1),
        "b_gate": nrm(ks[9], (DEPTH, 2 * D_MODEL), 0.02),
        "w_conv": nrm(ks[10], (DEPTH, CONV_WIDTH, D_CONV), CONV_WIDTH ** -0.5),
        "b_conv": nrm(ks[11], (DEPTH, D_CONV), 0.02),
        "w_pa": nrm(ks[12], (DEPTH, D_ATTN, D_MODEL), D_ATTN ** -0.5),
        "w_pc": nrm(ks[13], (DEPTH, D_CONV, D_MODEL), D_CONV ** -0.5),
        "w_out": nrm(ks[14], (DEPTH, D_MODEL, D_MODEL), D_MODEL ** -0.5),
        "g_post": 1.0 + nrm(ks[15], (DEPTH, D_MODEL), 0.02),
    }


def reference(x_prompt, x_sample, cache_k, cache_v, state_conv, page_table, meta_tokens,
              g_pre, w_in, b_sb, b_gate, w_conv, b_conv, w_pa, w_pc, w_out, g_post):
    bsz = x_prompt.shape[0]
    dec_b = x_sample.shape[0]
    meta = jnp.broadcast_to(meta_tokens[None].astype(x_prompt.dtype), (bsz, N_META, D_MODEL))
    xp = jnp.concatenate([meta, x_prompt], axis=1)
    xs = x_sample
    kp_l, vp_l, cp_l, ks_l, vs_l, cs_l = [], [], [], [], [], []
    for l in range(DEPTH):
        lw = (g_pre[l], w_in[l], b_sb[l], b_gate[l], w_conv[l], b_conv[l], w_pa[l], w_pc[l], w_out[l], g_post[l])
        zero_hist = jnp.zeros((bsz, CONV_WIDTH - 1, D_CONV), xp.dtype)
        xp, kp, vp, cp = mixer_layer(xp, zero_hist, attend_prompt, *lw)
        k_past = cache_k[l][page_table].reshape(dec_b, -1, N_HEADS, HEAD_DIM)
        v_past = cache_v[l][page_table].reshape(dec_b, -1, N_HEADS, HEAD_DIM)

        def attend_sample(q, k, v, bias, k_past=k_past, v_past=v_past):
            kf = jnp.concatenate([k_past.astype(k.dtype), k], axis=1)
            vf = jnp.concatenate([v_past.astype(v.dtype), v], axis=1)
            q_pos = k_past.shape[1] + jnp.arange(q.shape[1])
            k_pos = jnp.arange(kf.shape[1])
            return stick_breaking(q, kf, vf, q_pos, k_pos, bias)

        xs, ks_, vs_, cs_ = mixer_layer(xs, state_conv[l], attend_sample, *lw)
        kp_l.append(kp); vp_l.append(vp); cp_l.append(cp)
        ks_l.append(ks_); vs_l.append(vs_); cs_l.append(cs_)
    y_prompt = xp[:, N_META:]
    y_sample = xs
    return (y_prompt, y_sample, jnp.stack(kp_l), jnp.stack(vp_l), jnp.stack(cp_l),
            jnp.stack(ks_l), jnp.stack(vs_l), jnp.stack(cs_l))
```

```python
import functools

import jax
import jax.numpy as jnp
from jax import lax
from jax.experimental import pallas as pl
from jax.experimental.pallas import tpu as pltpu

HEAD_DIM = 128
CONV_WIDTH = 3
RMS_EPS = 1e-6
SEG = 1024
KEY_BLOCK = 256
PAGE = 128
VMEM_LIMIT = 56 * 1024 * 1024

F32 = jnp.float32
BF16 = jnp.bfloat16


def _proj_kernel(x_ref, g_ref, w_ref, o_ref, h_ref):
    @pl.when(pl.program_id(1) == 0)
    def _():
        x = x_ref[...]
        ms = jnp.mean(x * x, axis=-1, keepdims=True)
        h_ref[...] = (x * lax.rsqrt(ms + RMS_EPS) * g_ref[...]).astype(BF16)

    o_ref[...] = jnp.dot(h_ref[...], w_ref[...], preferred_element_type=F32)


def _proj(x2d, g_pre, w_in, tm, tn):
    m, d = x2d.shape
    n = w_in.shape[1]
    return pl.pallas_call(
        _proj_kernel,
        out_shape=jax.ShapeDtypeStruct((m, n), F32),
        grid=(m // tm, n // tn),
        in_specs=[
            pl.BlockSpec((tm, d), lambda i, j: (i, 0)),
            pl.BlockSpec((1, d), lambda i, j: (0, 0)),
            pl.BlockSpec((d, tn), lambda i, j: (0, j)),
        ],
        out_specs=pl.BlockSpec((tm, tn), lambda i, j: (i, j)),
        scratch_shapes=[pltpu.VMEM((tm, d), BF16)],
        compiler_params=pltpu.CompilerParams(
            dimension_semantics=("parallel", "arbitrary"),
            vmem_limit_bytes=VMEM_LIMIT),
        name="proj",
    )(x2d, g_pre.reshape(1, d), w_in)


def _neg_log_keep(z):
    return jnp.maximum(z, 0.0) + jnp.log1p(jnp.exp(-jnp.abs(z)))


def _suffix_weights(width):
    j = lax.broadcasted_iota(jnp.int32, (width, width), 0)
    s = lax.broadcasted_iota(jnp.int32, (width, width), 1)
    upper = jnp.where(j > s, -1.0, 0.0)
    return jnp.concatenate([upper, -jnp.ones((width, 128), F32)], axis=1).astype(BF16)


def _sb_block(qb, kb, vb, w, z_bias, carry, mask):
    width = kb.shape[0]
    d = lax.dot_general(qb, kb.astype(BF16), (((1,), (1,)), ((), ())),
                        preferred_element_type=F32)
    z = d * (HEAD_DIM ** -0.5) + z_bias
    nlk = _neg_log_keep(z)
    log_beta = z - nlk
    if mask is not None:
        nlk = jnp.where(mask, nlk, 0.0)
    res = jnp.dot(nlk.astype(BF16), w, preferred_element_type=F32)
    after = res[:, :width]
    total = res[:, width:]
    reps = width // 128
    carry_w = carry if reps == 1 else jnp.concatenate([carry] * reps, axis=1)
    p = jnp.exp(log_beta + after + carry_w)
    if mask is not None:
        p = jnp.where(mask, p, 0.0)
    out = jnp.dot(p.astype(BF16), vb.astype(BF16), preferred_element_type=F32)
    return out, carry + total


def _attn_prompt_kernel(bsb_ref, q_ref, k_ref, v_ref, km_ref, vm_ref, wk_ref, wm_ref,
                        o_ref, *, n_meta):
    h = pl.program_id(1)
    i = pl.program_id(2)
    tq = q_ref.shape[0]
    kb_sz = wk_ref.shape[0]
    z_bias = bsb_ref[h]
    qb = q_ref[...].astype(BF16)
    wk = wk_ref[...]

    t_loc = lax.broadcasted_iota(jnp.int32, (tq, kb_sz), 0)
    s_loc = lax.broadcasted_iota(jnp.int32, (tq, kb_sz), 1)
    row0 = pl.multiple_of(i * tq, tq)
    acc, carry = _sb_block(qb, k_ref[pl.ds(row0, kb_sz), :], v_ref[pl.ds(row0, kb_sz), :],
                           wk, z_bias, jnp.zeros((tq, 128), F32), s_loc < t_loc)

    def body(n, c):
        acc, carry = c
        r = pl.multiple_of((i - 1 - n) * kb_sz, kb_sz)
        out, carry = _sb_block(qb, k_ref[pl.ds(r, kb_sz), :], v_ref[pl.ds(r, kb_sz), :],
                               wk, z_bias, carry, None)
        return acc + out, carry

    acc, carry = lax.fori_loop(0, i, body, (acc, carry))

    s_m = lax.broadcasted_iota(jnp.int32, (tq, km_ref.shape[0]), 1)
    out, _ = _sb_block(qb, km_ref[...], vm_ref[...], wm_ref[...], z_bias, carry, s_m < n_meta)
    o_ref[...] = acc + out


def _attn_prompt(p2d, km_pad, vm_pad, b_sb, bsz, seq, n_heads, n_meta):
    tq = KEY_BLOCK
    nq = seq // tq
    kernel = functools.partial(_attn_prompt_kernel, n_meta=n_meta)
    return pl.pallas_call(
        kernel,
        out_shape=jax.ShapeDtypeStruct((bsz * seq, n_heads * HEAD_DIM), F32),
        grid=(bsz, n_heads, nq),
        in_specs=[
            pl.BlockSpec(memory_space=pltpu.SMEM),
            pl.BlockSpec((tq, HEAD_DIM), lambda b, h, i: (b * nq + i, h)),
            pl.BlockSpec((seq, HEAD_DIM), lambda b, h, i: (b, n_heads + h)),
            pl.BlockSpec((seq, HEAD_DIM), lambda b, h, i: (b, 2 * n_heads + h)),
            pl.BlockSpec((PAGE, HEAD_DIM), lambda b, h, i: (0, h)),
            pl.BlockSpec((PAGE, HEAD_DIM), lambda b, h, i: (0, h)),
            pl.BlockSpec((KEY_BLOCK, KEY_BLOCK + 128), lambda b, h, i: (0, 0)),
            pl.BlockSpec((PAGE, PAGE + 128), lambda b, h, i: (0, 0)),
        ],
        out_specs=pl.BlockSpec((tq, HEAD_DIM), lambda b, h, i: (b * nq + i, h)),
        compiler_params=pltpu.CompilerParams(
            dimension_semantics=("parallel", "parallel", "arbitrary"),
            vmem_limit_bytes=VMEM_LIMIT),
        name="attn_prompt",
    )(b_sb, p2d, p2d, p2d, km_pad, vm_pad, _suffix_weights(KEY_BLOCK), _suffix_weights(PAGE))


def _attn_sample_kernel(pt_ref, q_ref, kn_ref, vn_ref, bias_ref, hm_ref, w_ref, *rest,
                        n_pages, n_heads):
    k_refs = rest[:n_pages]
    v_refs = rest[n_pages:2 * n_pages]
    o_ref = rest[2 * n_pages]
    knp_ref, vnp_ref = rest[2 * n_pages + 1:]
    dec_seq = q_ref.shape[0]
    rows = dec_seq * n_heads

    q = q_ref[...]
    hm = hm_ref[...]
    qx = jnp.concatenate(
        [jnp.broadcast_to(q[t:t + 1, :], (n_heads, q.shape[1])) for t in range(dec_seq)],
        axis=0)
    qx = (qx * hm).astype(BF16)
    bias = bias_ref[...]
    w = w_ref[...]

    knp_ref[...] = jnp.zeros_like(knp_ref)
    vnp_ref[...] = jnp.zeros_like(vnp_ref)
    knp_ref[0:dec_seq, :] = kn_ref[...]
    vnp_ref[0:dec_seq, :] = vn_ref[...]

    def scores(k_page):
        d = lax.dot_general(qx, k_page.astype(BF16), (((1,), (1,)), ((), ())),
                            preferred_element_type=F32)
        return d * (HEAD_DIM ** -0.5) + bias

    t_idx = lax.broadcasted_iota(jnp.int32, (rows, PAGE), 0) // n_heads
    j_idx = lax.broadcasted_iota(jnp.int32, (rows, PAGE), 1)
    mask = j_idx < t_idx
    z = scores(knp_ref[...])
    nlk = _neg_log_keep(z)
    res = jnp.dot(jnp.where(mask, nlk, 0.0).astype(BF16), w, preferred_element_type=F32)
    p = jnp.where(mask, jnp.exp(z - nlk + res[:, :PAGE]), 0.0)
    out = jnp.dot(p.astype(BF16), vnp_ref[...].astype(BF16), preferred_element_type=F32)
    carry = res[:, PAGE:]

    for pg in range(n_pages - 1, -1, -1):
        z = scores(k_refs[pg][...])
        nlk = _neg_log_keep(z)
        res = jnp.dot(nlk.astype(BF16), w, preferred_element_type=F32)
        p = jnp.exp(z - nlk + res[:, :PAGE] + carry)
        out = out + jnp.dot(p.astype(BF16), v_refs[pg][...].astype(BF16),
                            preferred_element_type=F32)
        carry = carry + res[:, PAGE:]

    out = out * hm
    o_ref[...] = jnp.concatenate(
        [jnp.sum(out[t * n_heads:(t + 1) * n_heads, :], axis=0, keepdims=True)
         for t in range(dec_seq)], axis=0)


def _attn_sample(p3d, cache_k, cache_v, page_table, b_sb, n_heads, dec_b):
    dec_seq = p3d.shape[1]
    n_pages = page_table.shape[1]
    width = n_heads * HEAD_DIM
    rows = dec_seq * n_heads
    bias = jnp.broadcast_to(jnp.tile(b_sb, dec_seq)[:, None], (rows, PAGE)).astype(F32)
    col_head = jnp.arange(width, dtype=jnp.int32) // HEAD_DIM
    row_head = jnp.arange(rows, dtype=jnp.int32) % n_heads
    head_mask = (col_head[None, :] == row_head[:, None]).astype(F32)

    def page_spec(pg):
        return pl.BlockSpec((None, PAGE, width), lambda b, pt: (pt[b, pg], 0, 0))

    def seg_spec(seg):
        return pl.BlockSpec((None, dec_seq, width), lambda b, pt: (b, 0, seg))

    const = lambda shape: pl.BlockSpec(shape, lambda b, pt: (0,) * len(shape))
    kernel = functools.partial(_attn_sample_kernel, n_pages=n_pages, n_heads=n_heads)
    return pl.pallas_call(
        kernel,
        out_shape=jax.ShapeDtypeStruct((dec_b, dec_seq, width), F32),
        grid_spec=pltpu.PrefetchScalarGridSpec(
            num_scalar_prefetch=1,
            grid=(dec_b,),
            in_specs=[seg_spec(0), seg_spec(1), seg_spec(2),
                      const((rows, PAGE)), const((rows, width)), const((PAGE, PAGE + 128))]
                     + [page_spec(pg) for pg in range(n_pages)] * 2,
            out_specs=pl.BlockSpec((None, dec_seq, width), lambda b, pt: (b, 0, 0)),
            scratch_shapes=[pltpu.VMEM((PAGE, width), F32), pltpu.VMEM((PAGE, width), F32)],
        ),
        compiler_params=pltpu.CompilerParams(
            dimension_semantics=("parallel",), vmem_limit_bytes=VMEM_LIMIT),
        name="attn_sample",
    )(page_table, p3d, p3d, p3d, bias, head_mask, _suffix_weights(PAGE),
      *([cache_k] * n_pages), *([cache_v] * n_pages))


def _silu(x):
    return x * jax.nn.sigmoid(x)


def _mix_out_kernel(*refs, prompt_mode, dec_seq):
    (a_ref, ga_ref, cb_ref, cc_ref, ch_ref, gc_ref, ma_ref, mc_ref, x_ref,
     wconv_ref, bconv_ref, bga_ref, bgc_ref, wpa_ref, wpc_ref, wout_ref, gpost_ref,
     h1_ref, h2_ref) = refs[:19]
    if prompt_mode:
        y_ref, cs_ref, hist_ref = refs[19:]
    else:
        y_ref, u_ref = refs[19:]
    i = pl.program_id(1)
    tm = a_ref.shape[0]

    u = cc_ref[...] * ch_ref[...]
    row = lax.broadcasted_iota(jnp.int32, u.shape, 0)
    if prompt_mode:
        @pl.when(i == 0)
        def _():
            n_m = h1_ref.shape[0]
            hist_ref[...] = (h1_ref[...] * h2_ref[...])[n_m - 8:, :]

        hist = hist_ref[...]
        u1 = jnp.where(row >= 1, pltpu.roll(u, 1, 0), hist[7:8, :])
        u2 = jnp.where(row >= 2, pltpu.roll(u, 2, 0),
                       jnp.where(row == 0, hist[6:7, :], hist[7:8, :]))
        hist_ref[...] = u[tm - 8:, :]

        @pl.when(i == pl.num_programs(1) - 1)
        def _():
            cs_ref[...] = u[tm - (CONV_WIDTH - 1):, :]
    else:
        t = row % dec_seq
        u1 = jnp.where(t >= 1, pltpu.roll(u, 1, 0), h1_ref[...])
        u2 = jnp.where(t >= 2, pltpu.roll(u, 2, 0), h2_ref[...])
        u_ref[...] = u

    wc = wconv_ref[...]
    conv = bconv_ref[...] + wc[0:1, :] * u2 + wc[1:2, :] * u1 + wc[2:3, :] * u
    c = cb_ref[...] * conv * _silu(gc_ref[...])
    a = a_ref[...] * _silu(ga_ref[...])
    pa = jnp.dot(a.astype(BF16), wpa_ref[...], preferred_element_type=F32)
    pc = jnp.dot(c.astype(BF16), wpc_ref[...], preferred_element_type=F32)
    merged = (jax.nn.sigmoid(ma_ref[...] + bga_ref[...]) * pa
              + jax.nn.sigmoid(mc_ref[...] + bgc_ref[...]) * pc)
    o = jnp.dot(merged.astype(BF16), wout_ref[...], preferred_element_type=F32)
    ms = jnp.mean(o * o, axis=-1, keepdims=True)
    y_ref[...] = x_ref[...] + o * lax.rsqrt(ms + RMS_EPS) * gpost_ref[...]


def _mix_out(a2d, p2d, x2d, w_conv, b_conv, b_gate, w_pa, w_pc, w_out, g_post,
             hist1, hist2, hist_specs, n_b, tm, prompt_mode, dec_seq):
    m, d = x2d.shape
    ni = m // (n_b * tm)
    row = lambda b, i: b * ni + i
    seg = lambda s: pl.BlockSpec((tm, SEG), lambda b, i: (row(b, i), s))
    wide = lambda s: pl.BlockSpec((tm, d), lambda b, i: (row(b, i), s))
    const = lambda shape: pl.BlockSpec(shape, lambda b, i: (0,) * len(shape),
                                       pipeline_mode=pl.Buffered(1))
    in_specs = [
        pl.BlockSpec((tm, SEG), lambda b, i: (row(b, i), 0)),
        seg(3), seg(4), seg(5), seg(6), seg(7),
        wide(4), wide(5),
        wide(0),
        const((CONV_WIDTH, SEG)), const((1, SEG)), const((1, d)), const((1, d)),
        const((SEG, d)), const((SEG, d)), const((d, d)), const((1, d)),
    ] + hist_specs
    if prompt_mode:
        out_shape = (jax.ShapeDtypeStruct((m, d), F32),
                     jax.ShapeDtypeStruct((n_b, CONV_WIDTH - 1, SEG), F32))
        out_specs = (pl.BlockSpec((tm, d), lambda b, i: (row(b, i), 0)),
                     pl.BlockSpec((None, CONV_WIDTH - 1, SEG), lambda b, i: (b, 0, 0)))
        scratch = [pltpu.VMEM((8, SEG), F32)]
    else:
        out_shape = (jax.ShapeDtypeStruct((m, d), F32), jax.ShapeDtypeStruct((m, SEG), F32))
        out_specs = (pl.BlockSpec((tm, d), lambda b, i: (row(b, i), 0)),
                     pl.BlockSpec((tm, SEG), lambda b, i: (row(b, i), 0)))
        scratch = []
    kernel = functools.partial(_mix_out_kernel, prompt_mode=prompt_mode, dec_seq=dec_seq)
    return pl.pallas_call(
        kernel,
        out_shape=out_shape,
        grid=(n_b, ni),
        in_specs=in_specs,
        out_specs=out_specs,
        scratch_shapes=scratch,
        compiler_params=pltpu.CompilerParams(
            dimension_semantics=("parallel", "arbitrary"),
            vmem_limit_bytes=VMEM_LIMIT),
        name="mix_out_prompt" if prompt_mode else "mix_out_sample",
    )(a2d, p2d, p2d, p2d, p2d, p2d, p2d, p2d, x2d,
      w_conv, b_conv.reshape(1, SEG), b_gate[:d].reshape(1, d), b_gate[d:].reshape(1, d),
      w_pa, w_pc, w_out, g_post.reshape(1, d), hist1, hist2)


def kernel(x_prompt, x_sample, cache_k, cache_v, state_conv, page_table, meta_tokens,
           g_pre, w_in, b_sb, b_gate, w_conv, b_conv, w_pa, w_pc, w_out, g_post):
    depth = w_in.shape[0]
    assert depth == 1, "meta-token rows are only carried as keys/values/conv state"
    bsz, seq, d = x_prompt.shape
    dec_b, dec_seq, _ = x_sample.shape
    n_meta = meta_tokens.shape[0]
    n_heads = cache_k.shape[3]
    width = n_heads * HEAD_DIM
    assert width == SEG and w_conv.shape[2] == SEG and w_in.shape[2] == 12 * SEG

    w_in_b = w_in[0].astype(BF16)
    w_pa_b = w_pa[0].astype(BF16)
    w_pc_b = w_pc[0].astype(BF16)
    w_out_b = w_out[0].astype(BF16)

    n_s = dec_b * dec_seq
    p_p = _proj(x_prompt.reshape(bsz * seq, d), g_pre[0], w_in_b, min(1024, bsz * seq), SEG)
    x_sm = jnp.concatenate([x_sample.reshape(n_s, d), meta_tokens], axis=0)
    p_sm = _proj(x_sm, g_pre[0], w_in_b, n_s + n_meta, SEG)

    k_meta = p_sm[n_s:, SEG:2 * SEG]
    v_meta = p_sm[n_s:, 2 * SEG:3 * SEG]
    pad = ((0, PAGE - n_meta), (0, 0))
    a_p = _attn_prompt(p_p, jnp.pad(k_meta, pad), jnp.pad(v_meta, pad), b_sb[0],
                       bsz, seq, n_heads, n_meta)

    n_pool = cache_k.shape[1]
    a_s = _attn_sample(p_sm.reshape((n_s + n_meta) // dec_seq, dec_seq, 12 * SEG),
                       cache_k[0].reshape(n_pool, PAGE, width),
                       cache_v[0].reshape(n_pool, PAGE, width),
                       page_table, b_sb[0], n_heads, dec_b)

    lw = (w_conv[0], b_conv[0], b_gate[0], w_pa_b, w_pc_b, w_out_b, g_post[0])
    meta_blk = (n_s + n_meta) // n_meta - 1
    hist_specs = [pl.BlockSpec((n_meta, SEG), lambda b, i: (meta_blk, 5)),
                  pl.BlockSpec((n_meta, SEG), lambda b, i: (meta_blk, 6))]
    y_p, conv_p = _mix_out(a_p, p_p, x_prompt.reshape(bsz * seq, d), *lw,
                           p_sm, p_sm, hist_specs, bsz, 256, True, dec_seq)

    st = state_conv[0]
    zeros = jnp.zeros((dec_b, dec_seq - 1, SEG), F32)
    prev1 = jnp.concatenate([st[:, 1:2], zeros], axis=1).reshape(n_s, SEG)
    prev2 = jnp.concatenate([st, zeros[:, 1:]], axis=1).reshape(n_s, SEG)
    tm_s = min(256, n_s)
    hist_specs = [pl.BlockSpec((tm_s, SEG), lambda b, i: (i, 0))] * 2
    y_s, u_s = _mix_out(a_s.reshape(n_s, width), p_sm, x_sample.reshape(n_s, d), *lw,
                        prev1, prev2, hist_specs, 1, tm_s, False, dec_seq)

    def with_meta(cols, meta_rows):
        body = cols.reshape(bsz, seq, width)
        head = jnp.broadcast_to(meta_rows[None], (bsz, n_meta, width))
        return jnp.concatenate([head, body], axis=1).reshape(
            1, bsz, seq + n_meta, n_heads, HEAD_DIM)

    k_prompt = with_meta(p_p[:, SEG:2 * SEG], k_meta)
    v_prompt = with_meta(p_p[:, 2 * SEG:3 * SEG], v_meta)
    k_sample = p_sm[:n_s, SEG:2 * SEG].reshape(1, dec_b, dec_seq, n_heads, HEAD_DIM)
    v_sample = p_sm[:n_s, 2 * SEG:3 * SEG].reshape(1, dec_b, dec_seq, n_heads, HEAD_DIM)
    conv_sample = u_s.reshape(dec_b, dec_seq, SEG)[:, dec_seq - (CONV_WIDTH - 1):][None]
    return (y_p.reshape(bsz, seq, d), y_s.reshape(dec_b, dec_seq, d), k_prompt, v_prompt,
            conv_p[None], k_sample, v_sample, conv_sample)
```

```python
import functools

import jax
import jax.numpy as jnp
from jax import lax
from jax.experimental import pallas as pl
from jax.experimental.pallas import tpu as pltpu

HEAD_DIM = 128
LANES = 128
SUBLANES = 8
CONV_WIDTH = 3
RMS_EPS = 1e-6
SEG = 1024
N_SEG_OUT = 10
KEY_BLOCK = 256
HEADS_PER_STEP = 4
PAGE = 128
NEW_KEY_PAD = 16
VMEM_LIMIT = 56 * 1024 * 1024

F32 = jnp.float32
BF16 = jnp.bfloat16


def _proj_kernel(x_ref, g_ref, w_ref, o_ref, k_ref, v_ref, h_ref):
    j = pl.program_id(1)

    @pl.when(j == 0)
    def _():
        x = x_ref[...]
        ms = jnp.mean(x * x, axis=-1, keepdims=True)
        h_ref[...] = (x * lax.rsqrt(ms + RMS_EPS) * g_ref[...]).astype(BF16)

    def tile():
        return jnp.dot(h_ref[...], w_ref[...], preferred_element_type=F32)

    @pl.when(j == 1)
    def _():
        k_ref[...] = tile()

    @pl.when(j == 2)
    def _():
        v_ref[...] = tile()

    @pl.when((j == 0) | (j >= 3))
    def _():
        o_ref[...] = tile()


def _proj(x2d, g_pre, w_in, tm, kv_rows, kv_row_start):
    m, d = x2d.shape
    n_seg = w_in.shape[1] // SEG
    kv_spec = pl.BlockSpec((pl.Element(tm), pl.Element(SEG)),
                           lambda i, j: (pl.multiple_of(kv_row_start(i), SUBLANES), 0))
    return pl.pallas_call(
        _proj_kernel,
        out_shape=(jax.ShapeDtypeStruct((m, N_SEG_OUT * SEG), F32),
                   jax.ShapeDtypeStruct((kv_rows, SEG), F32),
                   jax.ShapeDtypeStruct((kv_rows, SEG), F32)),
        grid=(m // tm, n_seg),
        in_specs=[
            pl.BlockSpec((tm, d), lambda i, j: (i, 0)),
            pl.BlockSpec((1, d), lambda i, j: (0, 0)),
            pl.BlockSpec((d, SEG), lambda i, j: (0, j)),
        ],
        out_specs=(pl.BlockSpec((tm, SEG), lambda i, j: (i, jnp.maximum(j - 2, 0))),
                   kv_spec, kv_spec),
        scratch_shapes=[pltpu.VMEM((tm, d), BF16)],
        compiler_params=pltpu.CompilerParams(
            dimension_semantics=("parallel", "arbitrary"),
            vmem_limit_bytes=VMEM_LIMIT),
        name="proj",
    )(x2d, g_pre.reshape(1, d), w_in)


def _meta_rows_kernel(km_ref, vm_ref, k_in, v_in, k_ref, v_ref):
    del k_in, v_in
    k_ref[...] = km_ref[...]
    v_ref[...] = vm_ref[...]


def _meta_rows(k_full, v_full, k_meta, v_meta, bsz, rows_per_batch):
    n_meta = k_meta.shape[0]
    meta = pl.BlockSpec((n_meta, SEG), lambda b: (0, 0))
    front = pl.BlockSpec((pl.Element(n_meta), pl.Element(SEG)),
                         lambda b: (pl.multiple_of(b * rows_per_batch, SUBLANES), 0))
    return pl.pallas_call(
        _meta_rows_kernel,
        out_shape=(jax.ShapeDtypeStruct(k_full.shape, F32),
                   jax.ShapeDtypeStruct(v_full.shape, F32)),
        grid=(bsz,),
        in_specs=[meta, meta, pl.BlockSpec(memory_space=pl.ANY),
                  pl.BlockSpec(memory_space=pl.ANY)],
        out_specs=(front, front),
        input_output_aliases={2: 0, 3: 1},
        name="meta_rows",
    )(k_meta, v_meta, k_full, v_full)


LOG2E = 1.4426950408889634
SCORE_SCALE2 = HEAD_DIM ** -0.5 * LOG2E


def _softplus2(z2):
    sign = jnp.uint32(0x80000000)
    neg_abs = lax.bitcast_convert_type(lax.bitcast_convert_type(z2, jnp.uint32) | sign, F32)
    return jnp.maximum(z2, 0.0) + jnp.log2(1.0 + jnp.exp2(neg_abs))


def _neg_suffix_matrix(width):
    j = lax.broadcasted_iota(jnp.int32, (width, width), 0)
    s = lax.broadcasted_iota(jnp.int32, (width, width), 1)
    return jnp.where(j > s, -1.0, 0.0).astype(BF16)


def _sb_block(qb, kb, vb, w, z_bias, carry, mask):
    d = lax.dot_general(qb, kb.astype(BF16), (((1,), (1,)), ((), ())),
                        preferred_element_type=F32)
    z = d * SCORE_SCALE2 + z_bias
    nlk = _softplus2(z)
    log_beta = z - nlk
    if mask is not None:
        nlk = jnp.where(mask, nlk, 0.0)
    after = jnp.dot(nlk.astype(BF16), w, preferred_element_type=F32)
    p = jnp.exp2(log_beta + after + carry)
    if mask is not None:
        p = jnp.where(mask, p, 0.0)
    out = jnp.dot(p.astype(BF16), vb.astype(BF16), preferred_element_type=F32)
    return out, carry - jnp.sum(nlk, axis=-1, keepdims=True)


def _attn_prompt_kernel(bsb_ref, q_ref, k_ref, v_ref, km_ref, vm_ref, wk_ref, wm_ref,
                        o_ref, *, n_meta):
    hg = pl.program_id(1)
    i = pl.program_id(2)
    tq = q_ref.shape[0]
    kb_sz = wk_ref.shape[0]
    heads = range(HEADS_PER_STEP)
    lanes = [slice(hh * HEAD_DIM, (hh + 1) * HEAD_DIM) for hh in heads]
    z_bias = [bsb_ref[hg * HEADS_PER_STEP + hh] * LOG2E for hh in heads]
    qb = [q_ref[:, lanes[hh]].astype(BF16) for hh in heads]
    wk = wk_ref[...]

    t_loc = lax.broadcasted_iota(jnp.int32, (tq, kb_sz), 0)
    s_loc = lax.broadcasted_iota(jnp.int32, (tq, kb_sz), 1)
    row0 = pl.multiple_of(i * tq, tq)
    state = []
    for hh in heads:
        state += list(_sb_block(qb[hh], k_ref[pl.ds(row0, kb_sz), lanes[hh]],
                                v_ref[pl.ds(row0, kb_sz), lanes[hh]], wk, z_bias[hh],
                                jnp.zeros((tq, 1), F32), s_loc < t_loc))

    def body(n, st):
        r = pl.multiple_of((i - 1 - n) * kb_sz, kb_sz)
        new = []
        for hh in heads:
            out, carry = _sb_block(qb[hh], k_ref[pl.ds(r, kb_sz), lanes[hh]],
                                   v_ref[pl.ds(r, kb_sz), lanes[hh]], wk, z_bias[hh],
                                   st[2 * hh + 1], None)
            new += [st[2 * hh] + out, carry]
        return tuple(new)

    state = lax.fori_loop(0, i, body, tuple(state))

    s_m = lax.broadcasted_iota(jnp.int32, (tq, km_ref.shape[0]), 1)
    wm = wm_ref[...]
    for hh in heads:
        out, _ = _sb_block(qb[hh], km_ref[:, lanes[hh]], vm_ref[:, lanes[hh]], wm,
                           z_bias[hh], state[2 * hh + 1], s_m < n_meta)
        o_ref[:, lanes[hh]] = state[2 * hh] + out


def _attn_prompt(p2d, k_full, v_full, km_pad, vm_pad, b_sb, bsz, seq, n_heads, n_meta):
    tq = KEY_BLOCK
    nq = seq // tq
    gw = HEADS_PER_STEP * HEAD_DIM
    kv_spec = pl.BlockSpec((pl.Element(seq), pl.Element(gw)),
                           lambda b, g, i: (
                               pl.multiple_of(b * (seq + n_meta) + n_meta, SUBLANES),
                               pl.multiple_of(g * gw, LANES)))
    kernel = functools.partial(_attn_prompt_kernel, n_meta=n_meta)
    return pl.pallas_call(
        kernel,
        out_shape=jax.ShapeDtypeStruct((bsz * seq, n_heads * HEAD_DIM), F32),
        grid=(bsz, n_heads // HEADS_PER_STEP, nq),
        in_specs=[
            pl.BlockSpec(memory_space=pltpu.SMEM),
            pl.BlockSpec((tq, gw), lambda b, g, i: (b * nq + i, g)),
            kv_spec, kv_spec,
            pl.BlockSpec((PAGE, gw), lambda b, g, i: (0, g)),
            pl.BlockSpec((PAGE, gw), lambda b, g, i: (0, g)),
            pl.BlockSpec((KEY_BLOCK, KEY_BLOCK), lambda b, g, i: (0, 0)),
            pl.BlockSpec((PAGE, PAGE), lambda b, g, i: (0, 0)),
        ],
        out_specs=pl.BlockSpec((tq, gw), lambda b, g, i: (b * nq + i, g)),
        compiler_params=pltpu.CompilerParams(
            dimension_semantics=("parallel", "parallel", "arbitrary"),
            vmem_limit_bytes=VMEM_LIMIT),
        name="attn_prompt",
    )(b_sb, p2d, k_full, v_full, km_pad, vm_pad,
      _neg_suffix_matrix(KEY_BLOCK), _neg_suffix_matrix(PAGE))


def _attn_sample_kernel(pt_ref, q_ref, kn_ref, vn_ref, bias_ref, *rest, n_pages, group):
    k_refs = rest[:n_pages]
    v_refs = rest[n_pages:2 * n_pages]
    o_ref = rest[2 * n_pages]
    z_ref, n_ref, knp_ref, vnp_ref = rest[2 * n_pages + 1:]
    dec_seq, n_heads, _ = q_ref.shape
    rows = dec_seq * n_heads
    n_groups = n_pages // group
    page_rows = PAGE * n_heads
    scale = SCORE_SCALE2
    contract_last = (((1,), (1,)), ((), ()))

    qb = q_ref[...].reshape(rows, HEAD_DIM).astype(BF16)
    zero = jnp.zeros_like(qb)
    q_bd = jnp.concatenate(
        [jnp.concatenate([qb if c == r else zero for c in range(group)], axis=1)
         for r in range(group)], axis=0)
    q_rep = jnp.concatenate([qb] * group, axis=0)
    bias = bias_ref[...]

    sub = lax.broadcasted_iota(jnp.int32, (SUBLANES, LANES), 0)
    lane = lax.broadcasted_iota(jnp.int32, (SUBLANES, LANES), 1)
    same_head = (lane % n_heads) == sub
    lane_page = lane // rows
    lane_t = (lane // n_heads) % dec_seq

    for g in range(n_groups):
        k_cat = jnp.concatenate(
            [k_refs[g * group + c][...].reshape(page_rows, HEAD_DIM).astype(BF16)
             for c in range(group)], axis=1)
        zt = lax.dot_general(k_cat, q_bd, contract_last, preferred_element_type=F32)
        z = (zt * scale + bias).reshape(PAGE, SUBLANES, LANES)
        z_ref[g] = z
        n_ref[g] = jnp.where(same_head, _softplus2(z), 0.0)

    knp_ref[...] = jnp.zeros_like(knp_ref)
    vnp_ref[...] = jnp.zeros_like(vnp_ref)
    knp_ref[0:dec_seq] = kn_ref[...]
    vnp_ref[0:dec_seq] = vn_ref[...]
    kn = knp_ref[...].reshape(NEW_KEY_PAD * n_heads, HEAD_DIM).astype(BF16)
    ztn = lax.dot_general(kn, q_rep, contract_last, preferred_element_type=F32)
    zn = (ztn * scale + bias).reshape(NEW_KEY_PAD, SUBLANES, LANES)
    tp = lax.broadcasted_iota(jnp.int32, (NEW_KEY_PAD, SUBLANES, LANES), 0)
    mask_n = same_head & (tp < lane_t)
    nlk_n = jnp.where(mask_n, _softplus2(zn), 0.0)
    run = jnp.zeros((SUBLANES, LANES), F32)
    after_n = []
    for tpi in range(NEW_KEY_PAD - 1, -1, -1):
        after_n.append(run)
        run = run - nlk_n[tpi]
    after_n = jnp.stack(after_n[::-1], axis=0)
    p_n = jnp.where(mask_n, jnp.exp2(zn - nlk_n + after_n), 0.0)
    later = run

    def chain(i, runs):
        key = PAGE - 1 - i
        new = []
        for g in range(n_groups):
            n = n_ref[g, key]
            z_ref[g, key] = z_ref[g, key] - n + runs[g]
            new.append(runs[g] - n)
        return tuple(new)

    zero_v = jnp.zeros((SUBLANES, LANES), F32)
    totals = lax.fori_loop(0, PAGE, chain, (zero_v,) * n_groups, unroll=8)

    carries = [None] * n_groups
    for g in range(n_groups - 1, -1, -1):
        t = totals[g]
        excl = jnp.zeros_like(t)
        full = t
        for s in range(1, group):
            r = pltpu.roll(t, LANES - s * rows, 1)
            excl = excl + jnp.where(lane_page < group - s, r, 0.0)
            full = full + r
        carries[g] = later + excl
        later = later + full

    out = jnp.zeros((rows, HEAD_DIM), F32)
    for g in range(n_groups):
        p = jnp.where(same_head, jnp.exp2(z_ref[g] + carries[g]), 0.0)
        pt = p.reshape(page_rows, LANES).T.astype(BF16)
        for c in range(group):
            v_page = v_refs[g * group + c][...].reshape(page_rows, HEAD_DIM).astype(BF16)
            out = out + jnp.dot(pt[c * rows:(c + 1) * rows, :], v_page,
                                preferred_element_type=F32)
    pnt = p_n.reshape(NEW_KEY_PAD * n_heads, LANES).T.astype(BF16)
    vn = vnp_ref[...].reshape(NEW_KEY_PAD * n_heads, HEAD_DIM).astype(BF16)
    out = out + jnp.dot(pnt[0:rows, :], vn, preferred_element_type=F32)
    o_ref[...] = out.reshape(dec_seq, n_heads, HEAD_DIM)


def _attn_sample(q4, kn4, vn4, cache_k, cache_v, page_table, b_sb):
    dec_b, dec_seq, n_heads, _ = q4.shape
    n_pages = page_table.shape[1]
    rows = dec_seq * n_heads
    group = LANES // rows
    assert n_heads == SUBLANES and group * rows == LANES and n_pages % group == 0
    assert dec_seq <= NEW_KEY_PAD and cache_k.shape[2] == PAGE
    bias = jnp.tile(b_sb * LOG2E, LANES // n_heads).reshape(1, LANES).astype(F32)

    def page_spec(pg):
        return pl.BlockSpec((None, None, PAGE, n_heads, HEAD_DIM),
                            lambda b, pt: (0, pt[b, pg], 0, 0, 0))

    new_spec = pl.BlockSpec((None, dec_seq, n_heads, HEAD_DIM), lambda b, pt: (b, 0, 0, 0))
    kernel = functools.partial(_attn_sample_kernel, n_pages=n_pages, group=group)
    return pl.pallas_call(
        kernel,
        out_shape=jax.ShapeDtypeStruct(q4.shape, F32),
        grid_spec=pltpu.PrefetchScalarGridSpec(
            num_scalar_prefetch=1,
            grid=(dec_b,),
            in_specs=[new_spec, new_spec, new_spec,
                      pl.BlockSpec((1, LANES), lambda b, pt: (0, 0))]
                     + [page_spec(pg) for pg in range(n_pages)] * 2,
            out_specs=new_spec,
            scratch_shapes=[
                pltpu.VMEM((n_pages // group, PAGE, SUBLANES, LANES), F32),
                pltpu.VMEM((n_pages // group, PAGE, SUBLANES, LANES), F32),
                pltpu.VMEM((NEW_KEY_PAD, n_heads, HEAD_DIM), F32),
                pltpu.VMEM((NEW_KEY_PAD, n_heads, HEAD_DIM), F32)],
        ),
        compiler_params=pltpu.CompilerParams(
            dimension_semantics=("parallel",), vmem_limit_bytes=VMEM_LIMIT),
        name="attn_sample",
    )(page_table, q4, kn4, vn4, bias, *([cache_k] * n_pages), *([cache_v] * n_pages))


def _silu(x):
    return x * jax.nn.sigmoid(x)


def _mix_out_kernel(*refs, prompt_mode, dec_seq):
    (a_ref, ga_ref, cb_ref, cc_ref, ch_ref, gc_ref, ma_ref, mc_ref, x_ref,
     wconv_ref, bconv_ref, bga_ref, bgc_ref, wpa_ref, wpc_ref, wout_ref, gpost_ref,
     h1_ref, h2_ref) = refs[:19]
    if prompt_mode:
        y_ref, cs_ref, hist_ref = refs[19:]
    else:
        y_ref, u_ref = refs[19:]
    i = pl.program_id(1)
    tm = a_ref.shape[0]

    u = cc_ref[...] * ch_ref[...]
    row = lax.broadcasted_iota(jnp.int32, u.shape, 0)
    if prompt_mode:
        @pl.when(i == 0)
        def _():
            n_m = h1_ref.shape[0]
            hist_ref[...] = (h1_ref[...] * h2_ref[...])[n_m - 8:, :]

        hist = hist_ref[...]
        u1 = jnp.where(row >= 1, pltpu.roll(u, 1, 0), hist[7:8, :])
        u2 = jnp.where(row >= 2, pltpu.roll(u, 2, 0),
                       jnp.where(row == 0, hist[6:7, :], hist[7:8, :]))
        hist_ref[...] = u[tm - 8:, :]

        @pl.when(i == pl.num_programs(1) - 1)
        def _():
            cs_ref[...] = u[tm - (CONV_WIDTH - 1):, :]
    else:
        t = row % dec_seq
        u1 = jnp.where(t >= 1, pltpu.roll(u, 1, 0), h1_ref[...])
        u2 = jnp.where(t >= 2, pltpu.roll(u, 2, 0), h2_ref[...])
        u_ref[...] = u

    wc = wconv_ref[...]
    conv = bconv_ref[...] + wc[0:1, :] * u2 + wc[1:2, :] * u1 + wc[2:3, :] * u
    c = cb_ref[...] * conv * _silu(gc_ref[...])
    a = a_ref[...] * _silu(ga_ref[...])
    pa = jnp.dot(a.astype(BF16), wpa_ref[...], preferred_element_type=F32)
    pc = jnp.dot(c.astype(BF16), wpc_ref[...], preferred_element_type=F32)
    merged = (jax.nn.sigmoid(ma_ref[...] + bga_ref[...]) * pa
              + jax.nn.sigmoid(mc_ref[...] + bgc_ref[...]) * pc)
    o = jnp.dot(merged.astype(BF16), wout_ref[...], preferred_element_type=F32)
    ms = jnp.mean(o * o, axis=-1, keepdims=True)
    y_ref[...] = x_ref[...] + o * lax.rsqrt(ms + RMS_EPS) * gpost_ref[...]


def _mix_out(a2d, p2d, x2d, w_conv, b_conv, b_gate, w_pa, w_pc, w_out, g_post,
             hist1, hist2, hist_specs, n_b, tm, prompt_mode, dec_seq):
    m, d = x2d.shape
    ni = m // (n_b * tm)
    row = lambda b, i: b * ni + i
    seg = lambda s: pl.BlockSpec((tm, SEG), lambda b, i: (row(b, i), s))
    wide = lambda s: pl.BlockSpec((tm, d), lambda b, i: (row(b, i), s))
    const = lambda shape: pl.BlockSpec(shape, lambda b, i: (0,) * len(shape),
                                       pipeline_mode=pl.Buffered(1))
    in_specs = [
        pl.BlockSpec((tm, SEG), lambda b, i: (row(b, i), 0)),
        seg(1), seg(2), seg(3), seg(4), seg(5),
        wide(3), wide(4),
        wide(0),
        const((CONV_WIDTH, SEG)), const((1, SEG)), const((1, d)), const((1, d)),
        const((SEG, d)), const((SEG, d)), const((d, d)), const((1, d)),
    ] + hist_specs
    if prompt_mode:
        out_shape = (jax.ShapeDtypeStruct((m, d), F32),
                     jax.ShapeDtypeStruct((n_b, CONV_WIDTH - 1, SEG), F32))
        out_specs = (pl.BlockSpec((tm, d), lambda b, i: (row(b, i), 0)),
                     pl.BlockSpec((None, CONV_WIDTH - 1, SEG), lambda b, i: (b, 0, 0)))
        scratch = [pltpu.VMEM((8, SEG), F32)]
    else:
        out_shape = (jax.ShapeDtypeStruct((m, d), F32), jax.ShapeDtypeStruct((m, SEG), F32))
        out_specs = (pl.BlockSpec((tm, d), lambda b, i: (row(b, i), 0)),
                     pl.BlockSpec((tm, SEG), lambda b, i: (row(b, i), 0)))
        scratch = []
    kernel = functools.partial(_mix_out_kernel, prompt_mode=prompt_mode, dec_seq=dec_seq)
    return pl.pallas_call(
        kernel,
        out_shape=out_shape,
        grid=(n_b, ni),
        in_specs=in_specs,
        out_specs=out_specs,
        scratch_shapes=scratch,
        compiler_params=pltpu.CompilerParams(
            dimension_semantics=("parallel", "arbitrary"),
            vmem_limit_bytes=VMEM_LIMIT),
        name="mix_out_prompt" if prompt_mode else "mix_out_sample",
    )(a2d, p2d, p2d, p2d, p2d, p2d, p2d, p2d, x2d,
      w_conv, b_conv.reshape(1, SEG), b_gate[:d].reshape(1, d), b_gate[d:].reshape(1, d),
      w_pa, w_pc, w_out, g_post.reshape(1, d), hist1, hist2)


def kernel(x_prompt, x_sample, cache_k, cache_v, state_conv, page_table, meta_tokens,
           g_pre, w_in, b_sb, b_gate, w_conv, b_conv, w_pa, w_pc, w_out, g_post):
    depth = w_in.shape[0]
    assert depth == 1, "meta-token rows are only carried as keys/values/conv state"
    bsz, seq, d = x_prompt.shape
    dec_b, dec_seq, _ = x_sample.shape
    n_meta = meta_tokens.shape[0]
    n_heads = cache_k.shape[3]
    width = n_heads * HEAD_DIM
    assert width == SEG and w_conv.shape[2] == SEG and w_in.shape[2] == 12 * SEG

    w_in_b = w_in[0].astype(BF16)
    w_pa_b = w_pa[0].astype(BF16)
    w_pc_b = w_pc[0].astype(BF16)
    w_out_b = w_out[0].astype(BF16)

    n_s = dec_b * dec_seq
    x_sm = jnp.concatenate([x_sample.reshape(n_s, d), meta_tokens], axis=0)
    p_sm, k_sm, v_sm = _proj(x_sm, g_pre[0], w_in_b, n_s + n_meta, n_s + n_meta,
                             lambda i: i * (n_s + n_meta))
    k_meta, v_meta = k_sm[n_s:], v_sm[n_s:]

    tm_p = min(512, seq)
    tiles_per_batch = seq // tm_p
    rows_per_batch = seq + n_meta
    p_p, k_full, v_full = _proj(
        x_prompt.reshape(bsz * seq, d), g_pre[0], w_in_b, tm_p, bsz * rows_per_batch,
        lambda i: (i // tiles_per_batch) * rows_per_batch + n_meta
        + (i % tiles_per_batch) * tm_p)
    k_full, v_full = _meta_rows(k_full, v_full, k_meta, v_meta, bsz, rows_per_batch)

    pad = ((0, PAGE - n_meta), (0, 0))
    a_p = _attn_prompt(p_p, k_full, v_full, jnp.pad(k_meta, pad), jnp.pad(v_meta, pad),
                       b_sb[0], bsz, seq, n_heads, n_meta)

    new_shape = (dec_b, dec_seq, n_heads, HEAD_DIM)
    kn4 = k_sm[:n_s].reshape(new_shape)
    vn4 = v_sm[:n_s].reshape(new_shape)
    a_s = _attn_sample(p_sm[:n_s, :SEG].reshape(new_shape), kn4, vn4, cache_k, cache_v,
                       page_table, b_sb[0])

    lw = (w_conv[0], b_conv[0], b_gate[0], w_pa_b, w_pc_b, w_out_b, g_post[0])
    meta_blk = (n_s + n_meta) // n_meta - 1
    hist_specs = [pl.BlockSpec((n_meta, SEG), lambda b, i: (meta_blk, 3)),
                  pl.BlockSpec((n_meta, SEG), lambda b, i: (meta_blk, 4))]
    y_p, conv_p = _mix_out(a_p, p_p, x_prompt.reshape(bsz * seq, d), *lw,
                           p_sm, p_sm, hist_specs, bsz, 256, True, dec_seq)

    st = state_conv[0]
    zeros = jnp.zeros((dec_b, dec_seq - 1, SEG), F32)
    prev1 = jnp.concatenate([st[:, 1:2], zeros], axis=1).reshape(n_s, SEG)
    prev2 = jnp.concatenate([st, zeros[:, 1:]], axis=1).reshape(n_s, SEG)
    tm_s = min(256, n_s)
    hist_specs = [pl.BlockSpec((tm_s, SEG), lambda b, i: (i, 0))] * 2
    y_s, u_s = _mix_out(a_s.reshape(n_s, width), p_sm, x_sample.reshape(n_s, d), *lw,
                        prev1, prev2, hist_specs, 1, tm_s, False, dec_seq)

    kv_shape = (1, bsz, rows_per_batch, n_heads, HEAD_DIM)
    conv_sample = u_s.reshape(dec_b, dec_seq, SEG)[:, dec_seq - (CONV_WIDTH - 1):][None]
    return (y_p.reshape(bsz, seq, d), y_s.reshape(dec_b, dec_seq, d),
            k_full.reshape(kv_shape), v_full.reshape(kv_shape), conv_p[None],
            kn4[None], vn4[None], conv_sample)
```

```python
import functools

import jax
import jax.numpy as jnp
from jax import lax
from jax.experimental import pallas as pl
from jax.experimental.pallas import tpu as pltpu

HEAD_DIM = 128
LANES = 128
SUBLANES = 8
CONV_WIDTH = 3
RMS_EPS = 1e-6
SEG = 1024
N_SEG_OUT = 10
PROJ_COLS = 512
PROJ_ROWS = 1024
KEY_BLOCK = 256
HEADS_PER_STEP = 4
PAGE = 128
NEW_KEY_PAD = 16
VMEM_LIMIT = 56 * 1024 * 1024

F32 = jnp.float32
BF16 = jnp.bfloat16


def _proj_kernel(x_ref, g_ref, w_ref, o_ref, k_ref, v_ref, h_ref):
    j = pl.program_id(1)
    per_seg = SEG // PROJ_COLS

    @pl.when(j == 0)
    def _():
        x = x_ref[...]
        ms = jnp.mean(x * x, axis=-1, keepdims=True)
        h_ref[...] = (x * lax.rsqrt(ms + RMS_EPS) * g_ref[...]).astype(BF16)

    def tile():
        return jnp.dot(h_ref[...], w_ref[...], preferred_element_type=F32)

    @pl.when((j >= per_seg) & (j < 2 * per_seg))
    def _():
        k_ref[...] = tile()

    @pl.when((j >= 2 * per_seg) & (j < 3 * per_seg))
    def _():
        v_ref[...] = tile()

    @pl.when((j < per_seg) | (j >= 3 * per_seg))
    def _():
        o_ref[...] = tile()


def _proj(x2d, g_pre, w_in, tm, kv_rows, kv_row_start):
    m, d = x2d.shape
    tn = PROJ_COLS
    per_seg = SEG // tn

    def kv_spec(first_step):
        return pl.BlockSpec(
            (pl.Element(tm), pl.Element(tn)),
            lambda i, j: (pl.multiple_of(kv_row_start(i), SUBLANES),
                          pl.multiple_of(jnp.clip(j - first_step, 0, per_seg - 1) * tn, LANES)))

    def p_col(j):
        return jnp.where(j < per_seg, j, jnp.maximum(j - 2 * per_seg, per_seg - 1))

    return pl.pallas_call(
        _proj_kernel,
        out_shape=(jax.ShapeDtypeStruct((m, N_SEG_OUT * SEG), F32),
                   jax.ShapeDtypeStruct((kv_rows, SEG), F32),
                   jax.ShapeDtypeStruct((kv_rows, SEG), F32)),
        grid=(m // tm, w_in.shape[1] // tn),
        in_specs=[
            pl.BlockSpec((tm, d), lambda i, j: (i, 0)),
            pl.BlockSpec((1, d), lambda i, j: (0, 0)),
            pl.BlockSpec((d, tn), lambda i, j: (0, j)),
        ],
        out_specs=(pl.BlockSpec((tm, tn), lambda i, j: (i, p_col(j))),
                   kv_spec(per_seg), kv_spec(2 * per_seg)),
        scratch_shapes=[pltpu.VMEM((tm, d), BF16)],
        compiler_params=pltpu.CompilerParams(
            dimension_semantics=("parallel", "arbitrary"),
            vmem_limit_bytes=VMEM_LIMIT),
        name="proj",
    )(x2d, g_pre.reshape(1, d), w_in)


def _meta_rows_kernel(km_ref, vm_ref, k_in, v_in, k_ref, v_ref):
    del k_in, v_in
    k_ref[...] = km_ref[...]
    v_ref[...] = vm_ref[...]


def _meta_rows(k_full, v_full, k_meta, v_meta, bsz, rows_per_batch):
    n_meta = k_meta.shape[0]
    meta = pl.BlockSpec((n_meta, SEG), lambda b: (0, 0))
    front = pl.BlockSpec((pl.Element(n_meta), pl.Element(SEG)),
                         lambda b: (pl.multiple_of(b * rows_per_batch, SUBLANES), 0))
    return pl.pallas_call(
        _meta_rows_kernel,
        out_shape=(jax.ShapeDtypeStruct(k_full.shape, F32),
                   jax.ShapeDtypeStruct(v_full.shape, F32)),
        grid=(bsz,),
        in_specs=[meta, meta, pl.BlockSpec(memory_space=pl.ANY),
                  pl.BlockSpec(memory_space=pl.ANY)],
        out_specs=(front, front),
        input_output_aliases={2: 0, 3: 1},
        name="meta_rows",
    )(k_meta, v_meta, k_full, v_full)


LOG2E = 1.4426950408889634
SCORE_SCALE2 = HEAD_DIM ** -0.5 * LOG2E


def _softplus2(z2):
    sign = jnp.uint32(0x80000000)
    neg_abs = lax.bitcast_convert_type(lax.bitcast_convert_type(z2, jnp.uint32) | sign, F32)
    return jnp.maximum(z2, 0.0) + jnp.log2(1.0 + jnp.exp2(neg_abs))


def _neg_suffix_matrix(width):
    j = lax.broadcasted_iota(jnp.int32, (width, width), 0)
    s = lax.broadcasted_iota(jnp.int32, (width, width), 1)
    return jnp.where(j > s, -1.0, 0.0).astype(BF16)


def _sb_blocks(qbs, kbs, vbs, w, z_biases, carries, mask):
    n = len(qbs)
    ds = [lax.dot_general(qbs[h], kbs[h].astype(BF16), (((1,), (1,)), ((), ())),
                          preferred_element_type=F32) for h in range(n)]
    log_betas, nlks = [], []
    for h in range(n):
        z = ds[h] * SCORE_SCALE2 + z_biases[h]
        nlk = _softplus2(z)
        log_betas.append(z - nlk)
        nlks.append(nlk if mask is None else jnp.where(mask, nlk, 0.0))
    afters = [jnp.dot(nlks[h].astype(BF16), w, preferred_element_type=F32) for h in range(n)]
    ps = []
    for h in range(n):
        p = jnp.exp2(log_betas[h] + afters[h] + carries[h])
        ps.append(p if mask is None else jnp.where(mask, p, 0.0))
    outs = [jnp.dot(ps[h].astype(BF16), vbs[h].astype(BF16), preferred_element_type=F32)
            for h in range(n)]
    new_carries = [carries[h] - jnp.sum(nlks[h], axis=-1, keepdims=True) for h in range(n)]
    return outs, new_carries


def _attn_prompt_kernel(bsb_ref, q_ref, k_ref, v_ref, km_ref, vm_ref, wk_ref, wm_ref,
                        o_ref, acc_ref, *, n_meta):
    hg = pl.program_id(1)
    i = pl.program_id(2)
    tq = q_ref.shape[0]
    kb_sz = wk_ref.shape[0]
    heads = range(HEADS_PER_STEP)
    lanes = [slice(hh * HEAD_DIM, (hh + 1) * HEAD_DIM) for hh in heads]
    z_bias = [bsb_ref[hg * HEADS_PER_STEP + hh] * LOG2E for hh in heads]
    qb = [q_ref[:, lanes[hh]].astype(BF16) for hh in heads]
    wk = wk_ref[...]

    def kv_blocks(ref, r):
        return [ref[pl.ds(r, kb_sz), lanes[hh]] for hh in heads]

    t_loc = lax.broadcasted_iota(jnp.int32, (tq, kb_sz), 0)
    s_loc = lax.broadcasted_iota(jnp.int32, (tq, kb_sz), 1)
    row0 = pl.multiple_of(i * tq, tq)
    outs, carries = _sb_blocks(qb, kv_blocks(k_ref, row0), kv_blocks(v_ref, row0), wk, z_bias,
                               [jnp.zeros((tq, 1), F32)] * HEADS_PER_STEP, s_loc < t_loc)
    for hh in heads:
        acc_ref[hh] = outs[hh]

    def body(n, carries):
        r = pl.multiple_of((i - 1 - n) * kb_sz, kb_sz)
        outs, carries = _sb_blocks(qb, kv_blocks(k_ref, r), kv_blocks(v_ref, r), wk, z_bias,
                                   list(carries), None)
        for hh in heads:
            acc_ref[hh] += outs[hh]
        return tuple(carries)

    carries = lax.fori_loop(0, i, body, tuple(carries))

    s_m = lax.broadcasted_iota(jnp.int32, (tq, km_ref.shape[0]), 1)
    outs, _ = _sb_blocks(qb, [km_ref[:, lanes[hh]] for hh in heads],
                         [vm_ref[:, lanes[hh]] for hh in heads], wm_ref[...], z_bias,
                         list(carries), s_m < n_meta)
    for hh in heads:
        o_ref[:, lanes[hh]] = acc_ref[hh] + outs[hh]


def _attn_prompt(p2d, k_full, v_full, km_pad, vm_pad, b_sb, bsz, seq, n_heads, n_meta):
    tq = KEY_BLOCK
    nq = seq // tq
    gw = HEADS_PER_STEP * HEAD_DIM
    kv_spec = pl.BlockSpec((pl.Element(seq), pl.Element(gw)),
                           lambda b, g, i: (
                               pl.multiple_of(b * (seq + n_meta) + n_meta, SUBLANES),
                               pl.multiple_of(g * gw, LANES)))
    kernel = functools.partial(_attn_prompt_kernel, n_meta=n_meta)
    return pl.pallas_call(
        kernel,
        out_shape=jax.ShapeDtypeStruct((bsz * seq, n_heads * HEAD_DIM), F32),
        grid=(bsz, n_heads // HEADS_PER_STEP, nq),
        in_specs=[
            pl.BlockSpec(memory_space=pltpu.SMEM),
            pl.BlockSpec((tq, gw), lambda b, g, i: (b * nq + i, g)),
            kv_spec, kv_spec,
            pl.BlockSpec((PAGE, gw), lambda b, g, i: (0, g)),
            pl.BlockSpec((PAGE, gw), lambda b, g, i: (0, g)),
            pl.BlockSpec((KEY_BLOCK, KEY_BLOCK), lambda b, g, i: (0, 0)),
            pl.BlockSpec((PAGE, PAGE), lambda b, g, i: (0, 0)),
        ],
        out_specs=pl.BlockSpec((tq, gw), lambda b, g, i: (b * nq + i, g)),
        scratch_shapes=[pltpu.VMEM((HEADS_PER_STEP, tq, HEAD_DIM), F32)],
        compiler_params=pltpu.CompilerParams(
            dimension_semantics=("parallel", "parallel", "arbitrary"),
            vmem_limit_bytes=VMEM_LIMIT),
        name="attn_prompt",
    )(b_sb, p2d, k_full, v_full, km_pad, vm_pad,
      _neg_suffix_matrix(KEY_BLOCK), _neg_suffix_matrix(PAGE))


def _attn_sample_kernel(pt_ref, q_ref, kn_ref, vn_ref, bias_ref, *rest, n_pages, group):
    k_refs = rest[:n_pages]
    v_refs = rest[n_pages:2 * n_pages]
    o_ref = rest[2 * n_pages]
    z_ref, n_ref, knp_ref, vnp_ref = rest[2 * n_pages + 1:]
    dec_seq, n_heads, _ = q_ref.shape
    rows = dec_seq * n_heads
    n_groups = n_pages // group
    page_rows = PAGE * n_heads
    scale = SCORE_SCALE2
    contract_last = (((1,), (1,)), ((), ()))

    qb = q_ref[...].reshape(rows, HEAD_DIM).astype(BF16)
    zero = jnp.zeros_like(qb)
    q_bd = jnp.concatenate(
        [jnp.concatenate([qb if c == r else zero for c in range(group)], axis=1)
         for r in range(group)], axis=0)
    q_rep = jnp.concatenate([qb] * group, axis=0)
    bias = bias_ref[...]

    sub = lax.broadcasted_iota(jnp.int32, (SUBLANES, LANES), 0)
    lane = lax.broadcasted_iota(jnp.int32, (SUBLANES, LANES), 1)
    same_head = (lane % n_heads) == sub
    lane_page = lane // rows
    lane_t = (lane // n_heads) % dec_seq

    for g in range(n_groups):
        k_cat = jnp.concatenate(
            [k_refs[g * group + c][...].reshape(page_rows, HEAD_DIM).astype(BF16)
             for c in range(group)], axis=1)
        zt = lax.dot_general(k_cat, q_bd, contract_last, preferred_element_type=F32)
        z = (zt * scale + bias).reshape(PAGE, SUBLANES, LANES)
        z_ref[g] = z
        n_ref[g] = jnp.where(same_head, _softplus2(z), 0.0)

    knp_ref[...] = jnp.zeros_like(knp_ref)
    vnp_ref[...] = jnp.zeros_like(vnp_ref)
    knp_ref[0:dec_seq] = kn_ref[...]
    vnp_ref[0:dec_seq] = vn_ref[...]
    kn = knp_ref[...].reshape(NEW_KEY_PAD * n_heads, HEAD_DIM).astype(BF16)
    ztn = lax.dot_general(kn, q_rep, contract_last, preferred_element_type=F32)
    zn = (ztn * scale + bias).reshape(NEW_KEY_PAD, SUBLANES, LANES)
    tp = lax.broadcasted_iota(jnp.int32, (NEW_KEY_PAD, SUBLANES, LANES), 0)
    mask_n = same_head & (tp < lane_t)
    nlk_n = jnp.where(mask_n, _softplus2(zn), 0.0)
    run = jnp.zeros((SUBLANES, LANES), F32)
    after_n = []
    for tpi in range(NEW_KEY_PAD - 1, -1, -1):
        after_n.append(run)
        run = run - nlk_n[tpi]
    after_n = jnp.stack(after_n[::-1], axis=0)
    p_n = jnp.where(mask_n, jnp.exp2(zn - nlk_n + after_n), 0.0)
    later = run

    def chain(i, runs):
        key = PAGE - 1 - i
        new = []
        for g in range(n_groups):
            n = n_ref[g, key]
            z_ref[g, key] = z_ref[g, key] - n + runs[g]
            new.append(runs[g] - n)
        return tuple(new)

    zero_v = jnp.zeros((SUBLANES, LANES), F32)
    totals = lax.fori_loop(0, PAGE, chain, (zero_v,) * n_groups, unroll=8)

    carries = [None] * n_groups
    for g in range(n_groups - 1, -1, -1):
        t = totals[g]
        excl = jnp.zeros_like(t)
        full = t
        for s in range(1, group):
            r = pltpu.roll(t, LANES - s * rows, 1)
            excl = excl + jnp.where(lane_page < group - s, r, 0.0)
            full = full + r
        carries[g] = later + excl
        later = later + full

    out = jnp.zeros((rows, HEAD_DIM), F32)
    for g in range(n_groups):
        p = jnp.where(same_head, jnp.exp2(z_ref[g] + carries[g]), 0.0)
        pt = p.reshape(page_rows, LANES).T.astype(BF16)
        for c in range(group):
            v_page = v_refs[g * group + c][...].reshape(page_rows, HEAD_DIM).astype(BF16)
            out = out + jnp.dot(pt[c * rows:(c + 1) * rows, :], v_page,
                                preferred_element_type=F32)
    pnt = p_n.reshape(NEW_KEY_PAD * n_heads, LANES).T.astype(BF16)
    vn = vnp_ref[...].reshape(NEW_KEY_PAD * n_heads, HEAD_DIM).astype(BF16)
    out = out + jnp.dot(pnt[0:rows, :], vn, preferred_element_type=F32)
    o_ref[...] = out.reshape(dec_seq, n_heads, HEAD_DIM)


def _attn_sample(q4, kn4, vn4, cache_k, cache_v, page_table, b_sb):
    dec_b, dec_seq, n_heads, _ = q4.shape
    n_pages = page_table.shape[1]
    rows = dec_seq * n_heads
    group = LANES // rows
    assert n_heads == SUBLANES and group * rows == LANES and n_pages % group == 0
    assert dec_seq <= NEW_KEY_PAD and cache_k.shape[2] == PAGE
    bias = jnp.tile(b_sb * LOG2E, LANES // n_heads).reshape(1, LANES).astype(F32)

    def page_spec(pg):
        return pl.BlockSpec((None, None, PAGE, n_heads, HEAD_DIM),
                            lambda b, pt: (0, pt[b, pg], 0, 0, 0))

    new_spec = pl.BlockSpec((None, dec_seq, n_heads, HEAD_DIM), lambda b, pt: (b, 0, 0, 0))
    kernel = functools.partial(_attn_sample_kernel, n_pages=n_pages, group=group)
    return pl.pallas_call(
        kernel,
        out_shape=jax.ShapeDtypeStruct(q4.shape, F32),
        grid_spec=pltpu.PrefetchScalarGridSpec(
            num_scalar_prefetch=1,
            grid=(dec_b,),
            in_specs=[new_spec, new_spec, new_spec,
                      pl.BlockSpec((1, LANES), lambda b, pt: (0, 0))]
                     + [page_spec(pg) for pg in range(n_pages)] * 2,
            out_specs=new_spec,
            scratch_shapes=[
                pltpu.VMEM((n_pages // group, PAGE, SUBLANES, LANES), F32),
                pltpu.VMEM((n_pages // group, PAGE, SUBLANES, LANES), F32),
                pltpu.VMEM((NEW_KEY_PAD, n_heads, HEAD_DIM), F32),
                pltpu.VMEM((NEW_KEY_PAD, n_heads, HEAD_DIM), F32)],
        ),
        compiler_params=pltpu.CompilerParams(
            dimension_semantics=("parallel",), vmem_limit_bytes=VMEM_LIMIT),
        name="attn_sample",
    )(page_table, q4, kn4, vn4, bias, *([cache_k] * n_pages), *([cache_v] * n_pages))


def _silu(x):
    return x * jax.nn.sigmoid(x)


def _mix_out_kernel(*refs, prompt_mode, dec_seq):
    (a_ref, ga_ref, cb_ref, cc_ref, ch_ref, gc_ref, ma_ref, mc_ref, x_ref,
     wconv_ref, bconv_ref, bga_ref, bgc_ref, wpa_ref, wpc_ref, wout_ref, gpost_ref,
     h1_ref, h2_ref) = refs[:19]
    if prompt_mode:
        y_ref, cs_ref, hist_ref = refs[19:]
    else:
        y_ref, u_ref = refs[19:]
    i = pl.program_id(1)
    tm = a_ref.shape[0]

    u = cc_ref[...] * ch_ref[...]
    row = lax.broadcasted_iota(jnp.int32, u.shape, 0)
    if prompt_mode:
        @pl.when(i == 0)
        def _():
            n_m = h1_ref.shape[0]
            hist_ref[...] = (h1_ref[...] * h2_ref[...])[n_m - 8:, :]

        hist = hist_ref[...]
        u1 = jnp.where(row >= 1, pltpu.roll(u, 1, 0), hist[7:8, :])
        u2 = jnp.where(row >= 2, pltpu.roll(u, 2, 0),
                       jnp.where(row == 0, hist[6:7, :], hist[7:8, :]))
        hist_ref[...] = u[tm - 8:, :]

        @pl.when(i == pl.num_programs(1) - 1)
        def _():
            cs_ref[...] = u[tm - (CONV_WIDTH - 1):, :]
    else:
        t = row % dec_seq
        u1 = jnp.where(t >= 1, pltpu.roll(u, 1, 0), h1_ref[...])
        u2 = jnp.where(t >= 2, pltpu.roll(u, 2, 0), h2_ref[...])
        u_ref[...] = u

    wc = wconv_ref[...]
    conv = bconv_ref[...] + wc[0:1, :] * u2 + wc[1:2, :] * u1 + wc[2:3, :] * u
    c = cb_ref[...] * conv * _silu(gc_ref[...])
    a = a_ref[...] * _silu(ga_ref[...])
    pa = jnp.dot(a.astype(BF16), wpa_ref[...], preferred_element_type=F32)
    pc = jnp.dot(c.astype(BF16), wpc_ref[...], preferred_element_type=F32)
    merged = (jax.nn.sigmoid(ma_ref[...] + bga_ref[...]) * pa
              + jax.nn.sigmoid(mc_ref[...] + bgc_ref[...]) * pc)
    o = jnp.dot(merged.astype(BF16), wout_ref[...], preferred_element_type=F32)
    ms = jnp.mean(o * o, axis=-1, keepdims=True)
    y_ref[...] = x_ref[...] + o * lax.rsqrt(ms + RMS_EPS) * gpost_ref[...]


def _mix_out(a2d, p2d, x2d, w_conv, b_conv, b_gate, w_pa, w_pc, w_out, g_post,
             hist1, hist2, hist_specs, n_b, tm, prompt_mode, dec_seq):
    m, d = x2d.shape
    ni = m // (n_b * tm)
    row = lambda b, i: b * ni + i
    seg = lambda s: pl.BlockSpec((tm, SEG), lambda b, i: (row(b, i), s))
    wide = lambda s: pl.BlockSpec((tm, d), lambda b, i: (row(b, i), s))
    const = lambda shape: pl.BlockSpec(shape, lambda b, i: (0,) * len(shape),
                                       pipeline_mode=pl.Buffered(1))
    in_specs = [
        pl.BlockSpec((tm, SEG), lambda b, i: (row(b, i), 0)),
        seg(1), seg(2), seg(3), seg(4), seg(5),
        wide(3), wide(4),
        wide(0),
        const((CONV_WIDTH, SEG)), const((1, SEG)), const((1, d)), const((1, d)),
        const((SEG, d)), const((SEG, d)), const((d, d)), const((1, d)),
    ] + hist_specs
    if prompt_mode:
        out_shape = (jax.ShapeDtypeStruct((m, d), F32),
                     jax.ShapeDtypeStruct((n_b, CONV_WIDTH - 1, SEG), F32))
        out_specs = (pl.BlockSpec((tm, d), lambda b, i: (row(b, i), 0)),
                     pl.BlockSpec((None, CONV_WIDTH - 1, SEG), lambda b, i: (b, 0, 0)))
        scratch = [pltpu.VMEM((8, SEG), F32)]
    else:
        out_shape = (jax.ShapeDtypeStruct((m, d), F32), jax.ShapeDtypeStruct((m, SEG), F32))
        out_specs = (pl.BlockSpec((tm, d), lambda b, i: (row(b, i), 0)),
                     pl.BlockSpec((tm, SEG), lambda b, i: (row(b, i), 0)))
        scratch = []
    kernel = functools.partial(_mix_out_kernel, prompt_mode=prompt_mode, dec_seq=dec_seq)
    return pl.pallas_call(
        kernel,
        out_shape=out_shape,
        grid=(n_b, ni),
        in_specs=in_specs,
        out_specs=out_specs,
        scratch_shapes=scratch,
        compiler_params=pltpu.CompilerParams(
            dimension_semantics=("parallel", "arbitrary"),
            vmem_limit_bytes=VMEM_LIMIT),
        name="mix_out_prompt" if prompt_mode else "mix_out_sample",
    )(a2d, p2d, p2d, p2d, p2d, p2d, p2d, p2d, x2d,
      w_conv, b_conv.reshape(1, SEG), b_gate[:d].reshape(1, d), b_gate[d:].reshape(1, d),
      w_pa, w_pc, w_out, g_post.reshape(1, d), hist1, hist2)


def kernel(x_prompt, x_sample, cache_k, cache_v, state_conv, page_table, meta_tokens,
           g_pre, w_in, b_sb, b_gate, w_conv, b_conv, w_pa, w_pc, w_out, g_post):
    depth = w_in.shape[0]
    assert depth == 1, "meta-token rows are only carried as keys/values/conv state"
    bsz, seq, d = x_prompt.shape
    dec_b, dec_seq, _ = x_sample.shape
    n_meta = meta_tokens.shape[0]
    n_heads = cache_k.shape[3]
    width = n_heads * HEAD_DIM
    assert width == SEG and w_conv.shape[2] == SEG and w_in.shape[2] == 12 * SEG

    w_in_b = w_in[0].astype(BF16)
    w_pa_b = w_pa[0].astype(BF16)
    w_pc_b = w_pc[0].astype(BF16)
    w_out_b = w_out[0].astype(BF16)

    n_s = dec_b * dec_seq
    x_sm = jnp.concatenate([x_sample.reshape(n_s, d), meta_tokens], axis=0)
    p_sm, k_sm, v_sm = _proj(x_sm, g_pre[0], w_in_b, n_s + n_meta, n_s + n_meta,
                             lambda i: i * (n_s + n_meta))
    k_meta, v_meta = k_sm[n_s:], v_sm[n_s:]

    tm_p = min(PROJ_ROWS, seq)
    tiles_per_batch = seq // tm_p
    rows_per_batch = seq + n_meta
    p_p, k_full, v_full = _proj(
        x_prompt.reshape(bsz * seq, d), g_pre[0], w_in_b, tm_p, bsz * rows_per_batch,
        lambda i: (i // tiles_per_batch) * rows_per_batch + n_meta
        + (i % tiles_per_batch) * tm_p)
    k_full, v_full = _meta_rows(k_full, v_full, k_meta, v_meta, bsz, rows_per_batch)

    pad = ((0, PAGE - n_meta), (0, 0))
    a_p = _attn_prompt(p_p, k_full, v_full, jnp.pad(k_meta, pad), jnp.pad(v_meta, pad),
                       b_sb[0], bsz, seq, n_heads, n_meta)

    new_shape = (dec_b, dec_seq, n_heads, HEAD_DIM)
    kn4 = k_sm[:n_s].reshape(new_shape)
    vn4 = v_sm[:n_s].reshape(new_shape)
    a_s = _attn_sample(p_sm[:n_s, :SEG].reshape(new_shape), kn4, vn4, cache_k, cache_v,
                       page_table, b_sb[0])

    lw = (w_conv[0], b_conv[0], b_gate[0], w_pa_b, w_pc_b, w_out_b, g_post[0])
    meta_blk = (n_s + n_meta) // n_meta - 1
    hist_specs = [pl.BlockSpec((n_meta, SEG), lambda b, i: (meta_blk, 3)),
                  pl.BlockSpec((n_meta, SEG), lambda b, i: (meta_blk, 4))]
    y_p, conv_p = _mix_out(a_p, p_p, x_prompt.reshape(bsz * seq, d), *lw,
                           p_sm, p_sm, hist_specs, bsz, 256, True, dec_seq)

    st = state_conv[0]
    zeros = jnp.zeros((dec_b, dec_seq - 1, SEG), F32)
    prev1 = jnp.concatenate([st[:, 1:2], zeros], axis=1).reshape(n_s, SEG)
    prev2 = jnp.concatenate([st, zeros[:, 1:]], axis=1).reshape(n_s, SEG)
    tm_s = min(256, n_s)
    hist_specs = [pl.BlockSpec((tm_s, SEG), lambda b, i: (i, 0))] * 2
    y_s, u_s = _mix_out(a_s.reshape(n_s, width), p_sm, x_sample.reshape(n_s, d), *lw,
                        prev1, prev2, hist_specs, 1, tm_s, False, dec_seq)

    kv_shape = (1, bsz, rows_per_batch, n_heads, HEAD_DIM)
    conv_sample = u_s.reshape(dec_b, dec_seq, SEG)[:, dec_seq - (CONV_WIDTH - 1):][None]
    return (y_p.reshape(bsz, seq, d), y_s.reshape(dec_b, dec_seq, d),
            k_full.reshape(kv_shape), v_full.reshape(kv_shape), conv_p[None],
            kn4[None], vn4[None], conv_sample)
```

```python
import functools

import jax
import jax.numpy as jnp
from jax import lax
from jax.experimental import pallas as pl
from jax.experimental.pallas import tpu as pltpu

HEAD_DIM = 128
LANES = 128
SUBLANES = 8
CONV_WIDTH = 3
RMS_EPS = 1e-6
SEG = 1024
N_SEG_OUT = 10
PROJ_COLS = 1024
PROJ_ROWS = 1024
KEY_BLOCK = 256
HEADS_PER_STEP = 4
PAGE = 128
NEW_KEY_PAD = 16
VMEM_LIMIT = 56 * 1024 * 1024

F32 = jnp.float32
BF16 = jnp.bfloat16


def _proj_kernel(x_ref, g_ref, w_ref, o_ref, k_ref, v_ref, h_ref):
    j = pl.program_id(1)
    per_seg = SEG // PROJ_COLS

    @pl.when(j == 0)
    def _():
        x = x_ref[...]
        ms = jnp.mean(x * x, axis=-1, keepdims=True)
        h_ref[...] = (x * lax.rsqrt(ms + RMS_EPS) * g_ref[...]).astype(BF16)

    def tile():
        return jnp.dot(h_ref[...], w_ref[...], preferred_element_type=F32)

    @pl.when((j >= per_seg) & (j < 2 * per_seg))
    def _():
        k_ref[...] = tile()

    @pl.when((j >= 2 * per_seg) & (j < 3 * per_seg))
    def _():
        v_ref[...] = tile()

    @pl.when((j < per_seg) | (j >= 3 * per_seg))
    def _():
        o_ref[...] = tile()


def _proj(x2d, g_pre, w_in, tm, kv_rows, kv_row_start):
    m, d = x2d.shape
    tn = PROJ_COLS
    per_seg = SEG // tn

    def kv_spec(first_step):
        return pl.BlockSpec(
            (pl.Element(tm), pl.Element(tn)),
            lambda i, j: (pl.multiple_of(kv_row_start(i), SUBLANES),
                          pl.multiple_of(jnp.clip(j - first_step, 0, per_seg - 1) * tn, LANES)))

    def p_col(j):
        return jnp.where(j < per_seg, j, jnp.maximum(j - 2 * per_seg, per_seg - 1))

    return pl.pallas_call(
        _proj_kernel,
        out_shape=(jax.ShapeDtypeStruct((m, N_SEG_OUT * SEG), F32),
                   jax.ShapeDtypeStruct((kv_rows, SEG), F32),
                   jax.ShapeDtypeStruct((kv_rows, SEG), F32)),
        grid=(m // tm, w_in.shape[1] // tn),
        in_specs=[
            pl.BlockSpec((tm, d), lambda i, j: (i, 0)),
            pl.BlockSpec((1, d), lambda i, j: (0, 0)),
            pl.BlockSpec((d, tn), lambda i, j: (0, j)),
        ],
        out_specs=(pl.BlockSpec((tm, tn), lambda i, j: (i, p_col(j))),
                   kv_spec(per_seg), kv_spec(2 * per_seg)),
        scratch_shapes=[pltpu.VMEM((tm, d), BF16)],
        compiler_params=pltpu.CompilerParams(
            dimension_semantics=("parallel", "arbitrary"),
            vmem_limit_bytes=VMEM_LIMIT),
        name="proj",
    )(x2d, g_pre.reshape(1, d), w_in)


def _meta_rows_kernel(km_ref, vm_ref, k_in, v_in, k_ref, v_ref):
    del k_in, v_in
    k_ref[...] = km_ref[...]
    v_ref[...] = vm_ref[...]


def _meta_rows(k_full, v_full, k_meta, v_meta, bsz, rows_per_batch):
    n_meta = k_meta.shape[0]
    meta = pl.BlockSpec((n_meta, SEG), lambda b: (0, 0))
    front = pl.BlockSpec((pl.Element(n_meta), pl.Element(SEG)),
                         lambda b: (pl.multiple_of(b * rows_per_batch, SUBLANES), 0))
    return pl.pallas_call(
        _meta_rows_kernel,
        out_shape=(jax.ShapeDtypeStruct(k_full.shape, F32),
                   jax.ShapeDtypeStruct(v_full.shape, F32)),
        grid=(bsz,),
        in_specs=[meta, meta, pl.BlockSpec(memory_space=pl.ANY),
                  pl.BlockSpec(memory_space=pl.ANY)],
        out_specs=(front, front),
        input_output_aliases={2: 0, 3: 1},
        name="meta_rows",
    )(k_meta, v_meta, k_full, v_full)


LOG2E = 1.4426950408889634
SCORE_SCALE2 = HEAD_DIM ** -0.5 * LOG2E


def _softplus2(z2):
    sign = jnp.uint32(0x80000000)
    neg_abs = lax.bitcast_convert_type(lax.bitcast_convert_type(z2, jnp.uint32) | sign, F32)
    return jnp.maximum(z2, 0.0) + jnp.log2(1.0 + jnp.exp2(neg_abs))


def _neg_suffix_matrix(width):
    j = lax.broadcasted_iota(jnp.int32, (width, width), 0)
    s = lax.broadcasted_iota(jnp.int32, (width, width), 1)
    return jnp.where(j > s, -1.0, 0.0).astype(BF16)


def _sb_blocks(qbs, kbs, vbs, w, z_biases, carries, mask):
    n = len(qbs)
    ds = [lax.dot_general(qbs[h], kbs[h].astype(BF16), (((1,), (1,)), ((), ())),
                          preferred_element_type=F32) for h in range(n)]
    log_betas, nlks = [], []
    for h in range(n):
        z = ds[h] * SCORE_SCALE2 + z_biases[h]
        nlk = _softplus2(z)
        log_betas.append(z - nlk)
        nlks.append(nlk if mask is None else jnp.where(mask, nlk, 0.0))
    afters = [jnp.dot(nlks[h].astype(BF16), w, preferred_element_type=F32) for h in range(n)]
    ps = []
    for h in range(n):
        p = jnp.exp2(log_betas[h] + afters[h] + carries[h])
        ps.append(p if mask is None else jnp.where(mask, p, 0.0))
    outs = [jnp.dot(ps[h].astype(BF16), vbs[h].astype(BF16), preferred_element_type=F32)
            for h in range(n)]
    new_carries = [carries[h] - jnp.sum(nlks[h], axis=-1, keepdims=True) for h in range(n)]
    return outs, new_carries


def _attn_prompt_kernel(bsb_ref, q_ref, k_ref, v_ref, km_ref, vm_ref, wk_ref, wm_ref,
                        o_ref, acc_ref, *, n_meta):
    hg = pl.program_id(1)
    i = pl.program_id(2)
    tq = q_ref.shape[0]
    kb_sz = wk_ref.shape[0]
    heads = range(HEADS_PER_STEP)
    lanes = [slice(hh * HEAD_DIM, (hh + 1) * HEAD_DIM) for hh in heads]
    z_bias = [bsb_ref[hg * HEADS_PER_STEP + hh] * LOG2E for hh in heads]
    qb = [q_ref[:, lanes[hh]].astype(BF16) for hh in heads]
    wk = wk_ref[...]

    def kv_blocks(ref, r):
        return [ref[pl.ds(r, kb_sz), lanes[hh]] for hh in heads]

    t_loc = lax.broadcasted_iota(jnp.int32, (tq, kb_sz), 0)
    s_loc = lax.broadcasted_iota(jnp.int32, (tq, kb_sz), 1)
    row0 = pl.multiple_of(i * tq, tq)
    outs, carries = _sb_blocks(qb, kv_blocks(k_ref, row0), kv_blocks(v_ref, row0), wk, z_bias,
                               [jnp.zeros((tq, 1), F32)] * HEADS_PER_STEP, s_loc < t_loc)
    for hh in heads:
        acc_ref[hh] = outs[hh]

    def body(n, carries):
        r = pl.multiple_of((i - 1 - n) * kb_sz, kb_sz)
        outs, carries = _sb_blocks(qb, kv_blocks(k_ref, r), kv_blocks(v_ref, r), wk, z_bias,
                                   list(carries), None)
        for hh in heads:
            acc_ref[hh] += outs[hh]
        return tuple(carries)

    carries = lax.fori_loop(0, i, body, tuple(carries))

    s_m = lax.broadcasted_iota(jnp.int32, (tq, km_ref.shape[0]), 1)
    outs, _ = _sb_blocks(qb, [km_ref[:, lanes[hh]] for hh in heads],
                         [vm_ref[:, lanes[hh]] for hh in heads], wm_ref[...], z_bias,
                         list(carries), s_m < n_meta)
    for hh in heads:
        o_ref[:, lanes[hh]] = acc_ref[hh] + outs[hh]


def _attn_prompt(p2d, k_full, v_full, km_pad, vm_pad, b_sb, bsz, seq, n_heads, n_meta):
    tq = KEY_BLOCK
    nq = seq // tq
    gw = HEADS_PER_STEP * HEAD_DIM
    kv_spec = pl.BlockSpec((pl.Element(seq), pl.Element(gw)),
                           lambda b, g, i: (
                               pl.multiple_of(b * (seq + n_meta) + n_meta, SUBLANES),
                               pl.multiple_of(g * gw, LANES)))
    kernel = functools.partial(_attn_prompt_kernel, n_meta=n_meta)
    return pl.pallas_call(
        kernel,
        out_shape=jax.ShapeDtypeStruct((bsz * seq, n_heads * HEAD_DIM), F32),
        grid=(bsz, n_heads // HEADS_PER_STEP, nq),
        in_specs=[
            pl.BlockSpec(memory_space=pltpu.SMEM),
            pl.BlockSpec((tq, gw), lambda b, g, i: (b * nq + i, g)),
            kv_spec, kv_spec,
            pl.BlockSpec((PAGE, gw), lambda b, g, i: (0, g)),
            pl.BlockSpec((PAGE, gw), lambda b, g, i: (0, g)),
            pl.BlockSpec((KEY_BLOCK, KEY_BLOCK), lambda b, g, i: (0, 0)),
            pl.BlockSpec((PAGE, PAGE), lambda b, g, i: (0, 0)),
        ],
        out_specs=pl.BlockSpec((tq, gw), lambda b, g, i: (b * nq + i, g)),
        scratch_shapes=[pltpu.VMEM((HEADS_PER_STEP, tq, HEAD_DIM), F32)],
        compiler_params=pltpu.CompilerParams(
            dimension_semantics=("parallel", "parallel", "arbitrary"),
            vmem_limit_bytes=VMEM_LIMIT),
        name="attn_prompt",
    )(b_sb, p2d, k_full, v_full, km_pad, vm_pad,
      _neg_suffix_matrix(KEY_BLOCK), _neg_suffix_matrix(PAGE))


def _attn_sample_kernel(pt_ref, q_ref, kn_ref, vn_ref, bias_ref, *rest, n_pages, group):
    k_refs = rest[:n_pages]
    v_refs = rest[n_pages:2 * n_pages]
    o_ref = rest[2 * n_pages]
    z_ref, n_ref, knp_ref, vnp_ref = rest[2 * n_pages + 1:]
    dec_seq, n_heads, _ = q_ref.shape
    rows = dec_seq * n_heads
    n_groups = n_pages // group
    page_rows = PAGE * n_heads
    scale = SCORE_SCALE2
    contract_last = (((1,), (1,)), ((), ()))

    qb = q_ref[...].reshape(rows, HEAD_DIM).astype(BF16)
    zero = jnp.zeros_like(qb)
    q_bd = jnp.concatenate(
        [jnp.concatenate([qb if c == r else zero for c in range(group)], axis=1)
         for r in range(group)], axis=0)
    q_rep = jnp.concatenate([qb] * group, axis=0)
    bias = bias_ref[...]

    sub = lax.broadcasted_iota(jnp.int32, (SUBLANES, LANES), 0)
    lane = lax.broadcasted_iota(jnp.int32, (SUBLANES, LANES), 1)
    same_head = (lane % n_heads) == sub
    lane_page = lane // rows
    lane_t = (lane // n_heads) % dec_seq

    for g in range(n_groups):
        k_cat = jnp.concatenate(
            [k_refs[g * group + c][...].reshape(page_rows, HEAD_DIM).astype(BF16)
             for c in range(group)], axis=1)
        zt = lax.dot_general(k_cat, q_bd, contract_last, preferred_element_type=F32)
        z = (zt * scale + bias).reshape(PAGE, SUBLANES, LANES)
        z_ref[g] = z
        n_ref[g] = jnp.where(same_head, _softplus2(z), 0.0)

    knp_ref[...] = jnp.zeros_like(knp_ref)
    vnp_ref[...] = jnp.zeros_like(vnp_ref)
    knp_ref[0:dec_seq] = kn_ref[...]
    vnp_ref[0:dec_seq] = vn_ref[...]
    kn = knp_ref[...].reshape(NEW_KEY_PAD * n_heads, HEAD_DIM).astype(BF16)
    ztn = lax.dot_general(kn, q_rep, contract_last, preferred_element_type=F32)
    zn = (ztn * scale + bias).reshape(NEW_KEY_PAD, SUBLANES, LANES)
    tp = lax.broadcasted_iota(jnp.int32, (NEW_KEY_PAD, SUBLANES, LANES), 0)
    mask_n = same_head & (tp < lane_t)
    nlk_n = jnp.where(mask_n, _softplus2(zn), 0.0)
    run = jnp.zeros((SUBLANES, LANES), F32)
    after_n = []
    for tpi in range(NEW_KEY_PAD - 1, -1, -1):
        after_n.append(run)
        run = run - nlk_n[tpi]
    after_n = jnp.stack(after_n[::-1], axis=0)
    p_n = jnp.where(mask_n, jnp.exp2(zn - nlk_n + after_n), 0.0)
    later = run

    def chain(i, runs):
        key = PAGE - 1 - i
        new = []
        for g in range(n_groups):
            n = n_ref[g, key]
            z_ref[g, key] = z_ref[g, key] - n + runs[g]
            new.append(runs[g] - n)
        return tuple(new)

    zero_v = jnp.zeros((SUBLANES, LANES), F32)
    totals = lax.fori_loop(0, PAGE, chain, (zero_v,) * n_groups, unroll=8)

    carries = [None] * n_groups
    for g in range(n_groups - 1, -1, -1):
        t = totals[g]
        excl = jnp.zeros_like(t)
        full = t
        for s in range(1, group):
            r = pltpu.roll(t, LANES - s * rows, 1)
            excl = excl + jnp.where(lane_page < group - s, r, 0.0)
            full = full + r
        carries[g] = later + excl
        later = later + full

    out = jnp.zeros((rows, HEAD_DIM), F32)
    for g in range(n_groups):
        p = jnp.where(same_head, jnp.exp2(z_ref[g] + carries[g]), 0.0)
        pt = p.reshape(page_rows, LANES).T.astype(BF16)
        for c in range(group):
            v_page = v_refs[g * group + c][...].reshape(page_rows, HEAD_DIM).astype(BF16)
            out = out + jnp.dot(pt[c * rows:(c + 1) * rows, :], v_page,
                                preferred_element_type=F32)
    pnt = p_n.reshape(NEW_KEY_PAD * n_heads, LANES).T.astype(BF16)
    vn = vnp_ref[...].reshape(NEW_KEY_PAD * n_heads, HEAD_DIM).astype(BF16)
    out = out + jnp.dot(pnt[0:rows, :], vn, preferred_element_type=F32)
    o_ref[...] = out.reshape(dec_seq, n_heads, HEAD_DIM)


def _attn_sample(q4, kn4, vn4, cache_k, cache_v, page_table, b_sb):
    dec_b, dec_seq, n_heads, _ = q4.shape
    n_pages = page_table.shape[1]
    rows = dec_seq * n_heads
    group = LANES // rows
    assert n_heads == SUBLANES and group * rows == LANES and n_pages % group == 0
    assert dec_seq <= NEW_KEY_PAD and cache_k.shape[2] == PAGE
    bias = jnp.tile(b_sb * LOG2E, LANES // n_heads).reshape(1, LANES).astype(F32)

    def page_spec(pg):
        return pl.BlockSpec((None, None, PAGE, n_heads, HEAD_DIM),
                            lambda b, pt: (0, pt[b, pg], 0, 0, 0))

    new_spec = pl.BlockSpec((None, dec_seq, n_heads, HEAD_DIM), lambda b, pt: (b, 0, 0, 0))
    kernel = functools.partial(_attn_sample_kernel, n_pages=n_pages, group=group)
    return pl.pallas_call(
        kernel,
        out_shape=jax.ShapeDtypeStruct(q4.shape, F32),
        grid_spec=pltpu.PrefetchScalarGridSpec(
            num_scalar_prefetch=1,
            grid=(dec_b,),
            in_specs=[new_spec, new_spec, new_spec,
                      pl.BlockSpec((1, LANES), lambda b, pt: (0, 0))]
                     + [page_spec(pg) for pg in range(n_pages)] * 2,
            out_specs=new_spec,
            scratch_shapes=[
                pltpu.VMEM((n_pages // group, PAGE, SUBLANES, LANES), F32),
                pltpu.VMEM((n_pages // group, PAGE, SUBLANES, LANES), F32),
                pltpu.VMEM((NEW_KEY_PAD, n_heads, HEAD_DIM), F32),
                pltpu.VMEM((NEW_KEY_PAD, n_heads, HEAD_DIM), F32)],
        ),
        compiler_params=pltpu.CompilerParams(
            dimension_semantics=("parallel",), vmem_limit_bytes=VMEM_LIMIT),
        name="attn_sample",
    )(page_table, q4, kn4, vn4, bias, *([cache_k] * n_pages), *([cache_v] * n_pages))


def _silu(x):
    return x * jax.nn.sigmoid(x)


def _mix_out_kernel(*refs, prompt_mode, dec_seq):
    (a_ref, ga_ref, cb_ref, cc_ref, ch_ref, gc_ref, ma_ref, mc_ref, x_ref,
     wconv_ref, bconv_ref, bga_ref, bgc_ref, wpa_ref, wpc_ref, wout_ref, gpost_ref,
     h1_ref, h2_ref) = refs[:19]
    if prompt_mode:
        y_ref, cs_ref, hist_ref = refs[19:]
    else:
        y_ref, u_ref = refs[19:]
    i = pl.program_id(1)
    tm = a_ref.shape[0]

    u = cc_ref[...] * ch_ref[...]
    row = lax.broadcasted_iota(jnp.int32, u.shape, 0)
    if prompt_mode:
        @pl.when(i == 0)
        def _():
            n_m = h1_ref.shape[0]
            hist_ref[...] = (h1_ref[...] * h2_ref[...])[n_m - 8:, :]

        hist = hist_ref[...]
        u1 = jnp.where(row >= 1, pltpu.roll(u, 1, 0), hist[7:8, :])
        u2 = jnp.where(row >= 2, pltpu.roll(u, 2, 0),
                       jnp.where(row == 0, hist[6:7, :], hist[7:8, :]))
        hist_ref[...] = u[tm - 8:, :]

        @pl.when(i == pl.num_programs(1) - 1)
        def _():
            cs_ref[...] = u[tm - (CONV_WIDTH - 1):, :]
    else:
        t = row % dec_seq
        u1 = jnp.where(t >= 1, pltpu.roll(u, 1, 0), h1_ref[...])
        u2 = jnp.where(t >= 2, pltpu.roll(u, 2, 0), h2_ref[...])
        u_ref[...] = u

    wc = wconv_ref[...]
    conv = bconv_ref[...] + wc[0:1, :] * u2 + wc[1:2, :] * u1 + wc[2:3, :] * u
    c = cb_ref[...] * conv * _silu(gc_ref[...])
    a = a_ref[...] * _silu(ga_ref[...])
    pa = jnp.dot(a.astype(BF16), wpa_ref[...], preferred_element_type=F32)
    pc = jnp.dot(c.astype(BF16), wpc_ref[...], preferred_element_type=F32)
    merged = (jax.nn.sigmoid(ma_ref[...] + bga_ref[...]) * pa
              + jax.nn.sigmoid(mc_ref[...] + bgc_ref[...]) * pc)
    o = jnp.dot(merged.astype(BF16), wout_ref[...], preferred_element_type=F32)
    ms = jnp.mean(o * o, axis=-1, keepdims=True)
    y_ref[...] = x_ref[...] + o * lax.rsqrt(ms + RMS_EPS) * gpost_ref[...]


def _mix_out(a2d, p2d, x2d, w_conv, b_conv, b_gate, w_pa, w_pc, w_out, g_post,
             hist1, hist2, hist_specs, n_b, tm, prompt_mode, dec_seq):
    m, d = x2d.shape
    ni = m // (n_b * tm)
    row = lambda b, i: b * ni + i
    seg = lambda s: pl.BlockSpec((tm, SEG), lambda b, i: (row(b, i), s))
    wide = lambda s: pl.BlockSpec((tm, d), lambda b, i: (row(b, i), s))
    const = lambda shape: pl.BlockSpec(shape, lambda b, i: (0,) * len(shape),
                                       pipeline_mode=pl.Buffered(1))
    in_specs = [
        pl.BlockSpec((tm, SEG), lambda b, i: (row(b, i), 0)),
        seg(1), seg(2), seg(3), seg(4), seg(5),
        wide(3), wide(4),
        wide(0),
        const((CONV_WIDTH, SEG)), const((1, SEG)), const((1, d)), const((1, d)),
        const((SEG, d)), const((SEG, d)), const((d, d)), const((1, d)),
    ] + hist_specs
    if prompt_mode:
        out_shape = (jax.ShapeDtypeStruct((m, d), F32),
                     jax.ShapeDtypeStruct((n_b, CONV_WIDTH - 1, SEG), F32))
        out_specs = (pl.BlockSpec((tm, d), lambda b, i: (row(b, i), 0)),
                     pl.BlockSpec((None, CONV_WIDTH - 1, SEG), lambda b, i: (b, 0, 0)))
        scratch = [pltpu.VMEM((8, SEG), F32)]
    else:
        out_shape = (jax.ShapeDtypeStruct((m, d), F32), jax.ShapeDtypeStruct((m, SEG), F32))
        out_specs = (pl.BlockSpec((tm, d), lambda b, i: (row(b, i), 0)),
                     pl.BlockSpec((tm, SEG), lambda b, i: (row(b, i), 0)))
        scratch = []
    kernel = functools.partial(_mix_out_kernel, prompt_mode=prompt_mode, dec_seq=dec_seq)
    return pl.pallas_call(
        kernel,
        out_shape=out_shape,
        grid=(n_b, ni),
        in_specs=in_specs,
        out_specs=out_specs,
        scratch_shapes=scratch,
        compiler_params=pltpu.CompilerParams(
            dimension_semantics=("parallel", "arbitrary"),
            vmem_limit_bytes=VMEM_LIMIT),
        name="mix_out_prompt" if prompt_mode else "mix_out_sample",
    )(a2d, p2d, p2d, p2d, p2d, p2d, p2d, p2d, x2d,
      w_conv, b_conv.reshape(1, SEG), b_gate[:d].reshape(1, d), b_gate[d:].reshape(1, d),
      w_pa, w_pc, w_out, g_post.reshape(1, d), hist1, hist2)


def kernel(x_prompt, x_sample, cache_k, cache_v, state_conv, page_table, meta_tokens,
           g_pre, w_in, b_sb, b_gate, w_conv, b_conv, w_pa, w_pc, w_out, g_post):
    depth = w_in.shape[0]
    assert depth == 1, "meta-token rows are only carried as keys/values/conv state"
    bsz, seq, d = x_prompt.shape
    dec_b, dec_seq, _ = x_sample.shape
    n_meta = meta_tokens.shape[0]
    n_heads = cache_k.shape[3]
    width = n_heads * HEAD_DIM
    assert width == SEG and w_conv.shape[2] == SEG and w_in.shape[2] == 12 * SEG

    w_in_b = w_in[0].astype(BF16)
    w_pa_b = w_pa[0].astype(BF16)
    w_pc_b = w_pc[0].astype(BF16)
    w_out_b = w_out[0].astype(BF16)

    n_s = dec_b * dec_seq
    x_sm = jnp.concatenate([x_sample.reshape(n_s, d), meta_tokens], axis=0)
    p_sm, k_sm, v_sm = _proj(x_sm, g_pre[0], w_in_b, n_s + n_meta, n_s + n_meta,
                             lambda i: i * (n_s + n_meta))
    k_meta, v_meta = k_sm[n_s:], v_sm[n_s:]

    tm_p = min(PROJ_ROWS, seq)
    tiles_per_batch = seq // tm_p
    rows_per_batch = seq + n_meta
    p_p, k_full, v_full = _proj(
        x_prompt.reshape(bsz * seq, d), g_pre[0], w_in_b, tm_p, bsz * rows_per_batch,
        lambda i: (i // tiles_per_batch) * rows_per_batch + n_meta
        + (i % tiles_per_batch) * tm_p)
    k_full, v_full = _meta_rows(k_full, v_full, k_meta, v_meta, bsz, rows_per_batch)

    pad = ((0, PAGE - n_meta), (0, 0))
    a_p = _attn_prompt(p_p, k_full, v_full, jnp.pad(k_meta, pad), jnp.pad(v_meta, pad),
                       b_sb[0], bsz, seq, n_heads, n_meta)

    new_shape = (dec_b, dec_seq, n_heads, HEAD_DIM)
    kn4 = k_sm[:n_s].reshape(new_shape)
    vn4 = v_sm[:n_s].reshape(new_shape)
    a_s = _attn_sample(p_sm[:n_s, :SEG].reshape(new_shape), kn4, vn4, cache_k, cache_v,
                       page_table, b_sb[0])

    lw = (w_conv[0], b_conv[0], b_gate[0], w_pa_b, w_pc_b, w_out_b, g_post[0])
    meta_blk = (n_s + n_meta) // n_meta - 1
    hist_specs = [pl.BlockSpec((n_meta, SEG), lambda b, i: (meta_blk, 3)),
                  pl.BlockSpec((n_meta, SEG), lambda b, i: (meta_blk, 4))]
    y_p, conv_p = _mix_out(a_p, p_p, x_prompt.reshape(bsz * seq, d), *lw,
                           p_sm, p_sm, hist_specs, bsz, 256, True, dec_seq)

    st = state_conv[0]
    zeros = jnp.zeros((dec_b, dec_seq - 1, SEG), F32)
    prev1 = jnp.concatenate([st[:, 1:2], zeros], axis=1).reshape(n_s, SEG)
    prev2 = jnp.concatenate([st, zeros[:, 1:]], axis=1).reshape(n_s, SEG)
    tm_s = min(256, n_s)
    hist_specs = [pl.BlockSpec((tm_s, SEG), lambda b, i: (i, 0))] * 2
    y_s, u_s = _mix_out(a_s.reshape(n_s, width), p_sm, x_sample.reshape(n_s, d), *lw,
                        prev1, prev2, hist_specs, 1, tm_s, False, dec_seq)

    kv_shape = (1, bsz, rows_per_batch, n_heads, HEAD_DIM)
    conv_sample = u_s.reshape(dec_b, dec_seq, SEG)[:, dec_seq - (CONV_WIDTH - 1):][None]
    return (y_p.reshape(bsz, seq, d), y_s.reshape(dec_b, dec_seq, d),
            k_full.reshape(kv_shape), v_full.reshape(kv_shape), conv_p[None],
            kn4[None], vn4[None], conv_sample)
```
